```python
import math
import jax, jax.numpy as jnp
from jax import lax
import numpy as np

D_MODEL = 1024
BATCH = 8
SEQ = 2048
DEPTH = 4
DEC_BATCH = 128
DEC_SEQ = 8
PAST_LEN = 8192
PAGE_SIZE = 128

A_HEADS = 4
A_DK = 128
A_DV = 128
A_CHUNK = 32
B_HEADS = 4
B_KV_HEADS = 2
B_HEAD_DIM = 64
IDX_HEADS = 8
IDX_DIM = 64
TOPK_MAX = 256
C_HEADS = 4
C_NOPE = 64
C_ROPE = 32
C_V = 64
C_Q_LORA = 256
C_KV_LORA = 128
D_FF = -(-8 * D_MODEL // (3 * 256)) * 256
MIX_WIDTH = A_HEADS * A_DV + B_HEADS * B_HEAD_DIM + C_HEADS * C_V
ROPE_THETA = 10000.0
Q_BLOCK = 128
EPS = 1e-5
MASK_VALUE = -1e30
DEEPNORM_ALPHA = (2 * DEPTH) ** 0.25
DEEPNORM_BETA = (8 * DEPTH) ** -0.25
IN_SPLITS = (A_HEADS * A_DK, A_HEADS * A_DK, A_HEADS * A_DV, A_HEADS * A_DV,
             B_HEADS * B_HEAD_DIM, B_KV_HEADS * B_HEAD_DIM, B_KV_HEADS * B_HEAD_DIM,
             IDX_HEADS * IDX_DIM, IDX_DIM, IDX_HEADS,
             C_Q_LORA, C_KV_LORA, C_ROPE)

kernel_name = 'hymba_hgrn2_dsa_mla_decoder_step'

F32 = jnp.float32


def layer_norm(x, g, b):
    xf = x.astype(F32)
    mu = jnp.mean(xf, -1, keepdims=True)
    var = jnp.mean(jnp.square(xf - mu), -1, keepdims=True)
    return ((xf - mu) * lax.rsqrt(var + EPS) * g + b).astype(x.dtype)


def rms_norm(x, g):
    xf = x.astype(F32)
    return (xf * lax.rsqrt(jnp.mean(jnp.square(xf), -1, keepdims=True) + EPS) * g).astype(x.dtype)


def rope(x, pos):
    half = x.shape[-1] // 2
    inv = ROPE_THETA ** (-jnp.arange(half, dtype=F32) / half)
    ang = pos.astype(F32)[:, None] * inv[None, :]
    shape = (ang.shape[0],) + (1,) * (x.ndim - 3) + (half,)
    cos = jnp.cos(ang).reshape(shape)
    sin = jnp.sin(ang).reshape(shape)
    xf = x.astype(F32)
    x1, x2 = xf[..., :half], xf[..., half:]
    return jnp.concatenate([x1 * cos - x2 * sin, x2 * cos + x1 * sin], -1).astype(x.dtype)


def hgrn_lower_bounds(logits):
    p = jax.nn.softmax(logits.astype(F32), axis=0)
    return jnp.clip(jnp.cumsum(p, axis=0) - p[0:1], 0.0, 1.0 - 1e-4)


def to_blocks(t, qb):
    b, l = t.shape[:2]
    return t.reshape((b, l // qb, qb) + t.shape[2:]).swapaxes(0, 1)


def from_blocks(t):
    nb, b, qb = t.shape[:3]
    return t.swapaxes(0, 1).reshape((b, nb * qb) + t.shape[3:])


def project(h, w_in, pos, lb, cq_g, w_uq, ckv_g):
    b, l = h.shape[:2]
    offs = [int(o) for o in np.cumsum(IN_SPLITS)[:-1]]
    aq, af, ai, ag, bq, bk, bv, bqi, bki, bw, cdq, cdkv, ckr = jnp.split(h @ w_in, offs, axis=-1)
    z = af.astype(F32).reshape(b, l, A_HEADS, A_DK)
    lbh = lb.reshape(A_HEADS, A_DK)
    f = lbh + (1.0 - lbh) * jax.nn.sigmoid(z)
    a_logf = jnp.log(jnp.maximum(f, 1e-30))
    a_k = 1.0 - f
    a_q = aq.reshape(b, l, A_HEADS, A_DK)
    a_v = ai.reshape(b, l, A_HEADS, A_DV)
    b_q = rope(bq.reshape(b, l, B_HEADS, B_HEAD_DIM), pos)
    b_k = rope(bk.reshape(b, l, B_KV_HEADS, B_HEAD_DIM), pos)
    b_v = bv.reshape(b, l, B_KV_HEADS, B_HEAD_DIM)
    b_qi = rope(bqi.reshape(b, l, IDX_HEADS, IDX_DIM), pos)
    b_ki = rope(bki[:, :, None, :], pos)[:, :, 0]
    b_w = bw * (IDX_HEADS ** -0.5 * IDX_DIM ** -0.5)
    cq = jnp.einsum('blr,rhd->blhd', rms_norm(cdq, cq_g), w_uq)
    c_qn = cq[..., :C_NOPE]
    c_qr = rope(cq[..., C_NOPE:], pos)
    c_kv = rms_norm(cdkv, ckv_g)
    c_kr = rope(ckr[:, :, None, :], pos)[:, :, 0]
    return (a_q, a_logf, a_k, a_v, ag), (b_q, b_k, b_v, b_qi, b_ki, b_w), (c_qn, c_qr, c_kv, c_kr)


def hgrn2_recurrence(q, log_f, k, v, s0):
    b, l, h, _ = q.shape
    c = math.gcd(l, A_CHUNK)
    n = l // c

    def blocks(t):
        return t.astype(F32).reshape(b, n, c, h, t.shape[-1]).transpose(1, 0, 3, 2, 4)

    causal = jnp.tril(jnp.ones((c, c), bool))

    def step(s, inp):
        qc, lf, kc, vc = inp
        cum = jnp.cumsum(lf, axis=2)
        diff = cum[:, :, :, None, :] - cum[:, :, None, :, :]
        decay = jnp.exp(jnp.where(causal[:, :, None], diff, MASK_VALUE))
        attn = jnp.einsum('bhtd,bhtsd,bhsd->bhts', qc, decay, kc)
        o = attn @ vc + jnp.einsum('bhtd,bhdv->bhtv', qc * jnp.exp(cum), s)
        last = cum[:, :, -1:, :]
        s_new = jnp.exp(last)[:, :, 0, :, None] * s + jnp.einsum('bhsd,bhsv->bhdv', kc * jnp.exp(last - cum), vc)
        return s_new, o

    s_fin, o = lax.scan(step, s0.astype(F32), (blocks(q), blocks(log_f), blocks(k), blocks(v)))
    return o.transpose(1, 0, 3, 2, 4).reshape(b, l, h, -1), s_fin


def hgrn_output(o, g, norm_g):
    b, l = o.shape[:2]
    return rms_norm(o, norm_g).reshape(b, l, -1) * jax.nn.silu(g.astype(F32))


def indexer_scores(qi, w, ki, q_pos, k_pos):
    score = jnp.zeros(qi.shape[:2] + (ki.shape[1],), F32)
    for hh in range(IDX_HEADS):
        dots = jnp.einsum('btd,bsd->bts', qi[:, :, hh], ki).astype(F32)
        score = score + w[:, :, hh, None].astype(F32) * jax.nn.relu(dots)
    return jnp.where(k_pos[None, None, :] <= q_pos[None, :, None], score, MASK_VALUE)


def sparse_attention(q, kg, vg, valid):
    b, t = q.shape[:2]
    qg = q.reshape(b, t, B_KV_HEADS, B_HEADS // B_KV_HEADS, B_HEAD_DIM)
    s = jnp.einsum('btkgd,btjkd->btkgj', qg, kg).astype(F32) * (B_HEAD_DIM ** -0.5)
    s = jnp.where(valid[:, :, None, None, :], s, MASK_VALUE)
    p = jax.nn.softmax(s, axis=-1).astype(vg.dtype)
    return jnp.einsum('btkgj,btjkd->btkgd', p, vg).reshape(b, t, B_HEADS * B_HEAD_DIM)


def gather_rows(rows, idx):
    return jax.vmap(lambda r, i: r[i])(rows, idx)


def dsa_prompt(q, k, v, qi, ki, w, pos):
    l = q.shape[1]
    topk = min(TOPK_MAX, l // 4)
    qb = min(Q_BLOCK, l)

    def body(args):
        q_b, qi_b, w_b, qpos = args
        _, sel = lax.top_k(indexer_scores(qi_b, w_b, ki, qpos, pos), topk)
        return sparse_attention(q_b, gather_rows(k, sel), gather_rows(v, sel), sel <= qpos[None, :, None])

    out = lax.map(body, (to_blocks(q, qb), to_blocks(qi, qb), to_blocks(w, qb), pos.reshape(l // qb, qb)))
    return from_blocks(out)


def dsa_sample(q, k_new, v_new, qi, ki_new, w, q_pos, past, pool_k, pool_v, pool_idx, page_table):
    db, t = q.shape[:2]
    ki_all = jnp.concatenate([pool_idx[page_table].reshape(db, past, IDX_DIM), ki_new.astype(pool_idx.dtype)], 1)
    k_pos = jnp.arange(past + t, dtype=jnp.int32)
    topk = min(TOPK_MAX, (past + t) // 4)
    _, sel = lax.top_k(indexer_scores(qi, w, ki_all, q_pos, k_pos), topk)
    in_past = sel < past
    sel_p = jnp.minimum(sel, past - 1)
    phys_page = jnp.take_along_axis(page_table, (sel_p // PAGE_SIZE).reshape(db, -1), axis=1).reshape(sel.shape)
    phys = phys_page * PAGE_SIZE + sel_p % PAGE_SIZE
    sel_n = jnp.clip(sel - past, 0, t - 1)
    flat_k = pool_k.reshape((-1,) + pool_k.shape[2:])
    flat_v = pool_v.reshape((-1,) + pool_v.shape[2:])
    kg = jnp.where(in_past[..., None, None], flat_k[phys], gather_rows(k_new, sel_n).astype(flat_k.dtype))
    vg = jnp.where(in_past[..., None, None], flat_v[phys], gather_rows(v_new, sel_n).astype(flat_v.dtype))
    return sparse_attention(q, kg, vg, sel <= q_pos[None, :, None])


def mla_attend(q_abs, q_rope, c_kv, k_rope, q_pos, k_pos, w_uv):
    b, t = q_abs.shape[:2]
    s = (jnp.einsum('bthr,bsr->bhts', q_abs, c_kv) + jnp.einsum('bthp,bsp->bhts', q_rope, k_rope)).astype(F32)
    s = s * ((C_NOPE + C_ROPE) ** -0.5)
    s = jnp.where(k_pos[None, None, None, :] <= q_pos[None, None, :, None], s, MASK_VALUE)
    p = jax.nn.softmax(s, axis=-1).astype(c_kv.dtype)
    o_lat = jnp.einsum('bhts,bsr->bthr', p, c_kv)
    return jnp.einsum('bthr,rhv->bthv', o_lat, w_uv).reshape(b, t, C_HEADS * C_V)


def mla_prompt(q_abs, q_rope, c_kv, k_rope, pos, w_uv):
    l = q_abs.shape[1]
    qb = min(Q_BLOCK, l)

    def body(args):
        qa, qr, qp = args
        return mla_attend(qa, qr, c_kv, k_rope, qp, pos, w_uv)

    return from_blocks(lax.map(body, (to_blocks(q_abs, qb), to_blocks(q_rope, qb), pos.reshape(l // qb, qb))))


def prompt_mixers(h, pos, w_in, lb, a_norm_g, cq_g, w_uq, ckv_g, w_uk, w_uv):
    (a_q, a_logf, a_k, a_v, a_g), (b_q, b_k, b_v, b_qi, b_ki, b_w), (c_qn, c_qr, c_kv, c_kr) = project(
        h, w_in, pos, lb, cq_g, w_uq, ckv_g)
    s0 = jnp.zeros((h.shape[0], A_HEADS, A_DK, A_DV), F32)
    o_a, s_a = hgrn2_recurrence(a_q, a_logf, a_k, a_v, s0)
    o_a = hgrn_output(o_a, a_g, a_norm_g)
    o_b = dsa_prompt(b_q, b_k, b_v, b_qi, b_ki, b_w, pos)
    q_abs = jnp.einsum('bthn,rhn->bthr', c_qn, w_uk)
    o_c = mla_prompt(q_abs, c_qr, c_kv, c_kr, pos, w_uv)
    mix = jnp.concatenate([o_a.astype(h.dtype), o_b.astype(h.dtype), o_c.astype(h.dtype)], -1)
    return mix, (b_k, b_v, b_ki, c_kv, c_kr, s_a)


def sample_mixers(h, pos, past, pool_bk, pool_bv, pool_bidx, pool_ckv, pool_ckr, s_prev, page_table,
                  w_in, lb, a_norm_g, cq_g, w_uq, ckv_g, w_uk, w_uv):
    db = h.shape[0]
    (a_q, a_logf, a_k, a_v, a_g), (b_q, b_k, b_v, b_qi, b_ki, b_w), (c_qn, c_qr, c_kv, c_kr) = project(
        h, w_in, pos, lb, cq_g, w_uq, ckv_g)
    o_a, s_a = hgrn2_recurrence(a_q, a_logf, a_k, a_v, s_prev)
    o_a = hgrn_output(o_a, a_g, a_norm_g)
    o_b = dsa_sample(b_q, b_k, b_v, b_qi, b_ki, b_w, pos, past, pool_bk, pool_bv, pool_bidx, page_table)
    c_all = jnp.concatenate([pool_ckv[page_table].reshape(db, past, C_KV_LORA), c_kv.astype(pool_ckv.dtype)], 1)
    kr_all = jnp.concatenate([pool_ckr[page_table].reshape(db, past, C_ROPE), c_kr.astype(pool_ckr.dtype)], 1)
    q_abs = jnp.einsum('bthn,rhn->bthr', c_qn, w_uk)
    k_pos = jnp.arange(past + h.shape[1], dtype=jnp.int32)
    o_c = mla_attend(q_abs, c_qr, c_all, kr_all, pos, k_pos, w_uv)
    mix = jnp.concatenate([o_a.astype(h.dtype), o_b.astype(h.dtype), o_c.astype(h.dtype)], -1)
    return mix, (b_k, b_v, b_ki, c_kv, c_kr, s_a)


def finish_layer(x, mix, w_o, ln1_g, ln1_b, w_gate, w_up, w_down, ln2_g, ln2_b):
    x = layer_norm(DEEPNORM_ALPHA * x + mix @ w_o, ln1_g, ln1_b)
    ffn = (jax.nn.silu(x @ w_gate) * (x @ w_up)) @ w_down
    return layer_norm(DEEPNORM_ALPHA * x + ffn, ln2_g, ln2_b)


def setup_inputs(seed: int = 0) -> dict:
    key = jax.random.key(seed)
    ks = jax.random.split(key, 32)
    n_pages = PAST_LEN // PAGE_SIZE
    n_used = DEC_BATCH * n_pages
    n_phys = n_used + n_used // 4
    in_width = sum(IN_SPLITS)

    def nrm(k, shape, scale=1.0):
        return jax.random.normal(k, shape, F32) * scale

    page_table = jax.random.permutation(ks[8], n_phys)[:n_used].reshape(DEC_BATCH, n_pages).astype(jnp.int32)
    return {
        'x_prompt': nrm(ks[0], (BATCH, SEQ, D_MODEL)),
        'x_sample': nrm(ks[1], (DEC_BATCH, DEC_SEQ, D_MODEL)),
        'cache_b_k': nrm(ks[2], (DEPTH, n_phys, PAGE_SIZE, B_KV_HEADS, B_HEAD_DIM)),
        'cache_b_v': nrm(ks[3], (DEPTH, n_phys, PAGE_SIZE, B_KV_HEADS, B_HEAD_DIM)),
        'cache_b_idx': nrm(ks[4], (DEPTH, n_phys, PAGE_SIZE, IDX_DIM)),
        'cache_c_kv': nrm(ks[5], (DEPTH, n_phys, PAGE_SIZE, C_KV_LORA)),
        'cache_c_kr': nrm(ks[6], (DEPTH, n_phys, PAGE_SIZE, C_ROPE)),
        'state_a': nrm(ks[7], (DEPTH, DEC_BATCH, A_HEADS, A_DK, A_DV), 0.5),
        'page_table': page_table,
        'ln_in_g': 1.0 + nrm(ks[9], (D_MODEL,), 0.01),
        'ln_in_b': nrm(ks[10], (D_MODEL,), 0.01),
        'w_in': nrm(ks[11], (DEPTH, D_MODEL, in_width), D_MODEL ** -0.5),
        'hgrn_lb_logits': nrm(ks[12], (DEPTH, A_HEADS * A_DK), 0.5),
        'hgrn_norm_g': 1.0 + nrm(ks[13], (DEPTH, A_HEADS, A_DV), 0.01),
        'mla_q_norm_g': 1.0 + nrm(ks[14], (DEPTH, C_Q_LORA), 0.01),
        'mla_w_uq': nrm(ks[15], (DEPTH, C_Q_LORA, C_HEADS, C_NOPE + C_ROPE), C_Q_LORA ** -0.5),
        'mla_kv_norm_g': 1.0 + nrm(ks[16], (DEPTH, C_KV_LORA), 0.01),
        'mla_w_uk': nrm(ks[17], (DEPTH, C_KV_LORA, C_HEADS, C_NOPE), C_KV_LORA ** -0.5),
        'mla_w_uv': nrm(ks[18], (DEPTH, C_KV_LORA, C_HEADS, C_V), C_KV_LORA ** -0.5),
        'w_o': nrm(ks[19], (DEPTH, MIX_WIDTH, D_MODEL), MIX_WIDTH ** -0.5 * DEEPNORM_BETA),
        'ln1_g': 1.0 + nrm(ks[20], (DEPTH, D_MODEL), 0.01),
        'ln1_b': nrm(ks[21], (DEPTH, D_MODEL), 0.01),
        'w_gate': nrm(ks[22], (DEPTH, D_MODEL, D_FF), D_MODEL ** -0.5),
        'w_up': nrm(ks[23], (DEPTH, D_MODEL, D_FF), D_MODEL ** -0.5),
        'w_down': nrm(ks[24], (DEPTH, D_FF, D_MODEL), D_FF ** -0.5 * DEEPNORM_BETA),
        'ln2_g': 1.0 + nrm(ks[25], (DEPTH, D_MODEL), 0.01),
        'ln2_b': nrm(ks[26], (DEPTH, D_MODEL), 0.01),
    }


def reference(x_prompt, x_sample, cache_b_k, cache_b_v, cache_b_idx, cache_c_kv, cache_c_kr, state_a,
              page_table, ln_in_g, ln_in_b, w_in, hgrn_lb_logits, hgrn_norm_g, mla_q_norm_g, mla_w_uq,
              mla_kv_norm_g, mla_w_uk, mla_w_uv, w_o, ln1_g, ln1_b, w_gate, w_up, w_down, ln2_g, ln2_b):
    lower_bounds = hgrn_lower_bounds(hgrn_lb_logits)
    past = page_table.shape[1] * PAGE_SIZE
    pos_p = jnp.arange(x_prompt.shape[1], dtype=jnp.int32)
    pos_s = past + jnp.arange(x_sample.shape[1], dtype=jnp.int32)
    xp = layer_norm(x_prompt, ln_in_g, ln_in_b)
    xs = layer_norm(x_sample, ln_in_g, ln_in_b)
    new_p = ([], [], [], [], [], [])
    new_s = ([], [], [], [], [], [])
    for l in range(DEPTH):
        mix_w = (w_in[l], lower_bounds[l], hgrn_norm_g[l], mla_q_norm_g[l], mla_w_uq[l],
                 mla_kv_norm_g[l], mla_w_uk[l], mla_w_uv[l])
        ffn_w = (w_o[l], ln1_g[l], ln1_b[l], w_gate[l], w_up[l], w_down[l], ln2_g[l], ln2_b[l])
        mix_p, st_p = prompt_mixers(xp, pos_p, *mix_w)
        mix_s, st_s = sample_mixers(xs, pos_s, past, cache_b_k[l], cache_b_v[l], cache_b_idx[l],
                                    cache_c_kv[l], cache_c_kr[l], state_a[l], page_table, *mix_w)
        xp = finish_layer(xp, mix_p, *ffn_w)
        xs = finish_layer(xs, mix_s, *ffn_w)
        for lst, st in zip(new_p, st_p):
            lst.append(st)
        for lst, st in zip(new_s, st_s):
            lst.append(st)
    b_k_p, b_v_p, b_idx_p, c_kv_p, c_kr_p, s_a_p = [jnp.stack(v, 0) for v in new_p]
    b_k_s, b_v_s, b_idx_s, c_kv_s, c_kr_s, s_a_s = [jnp.stack(v, 0) for v in new_s]
    return (xp, xs, b_k_p, b_v_p, b_idx_p, c_kv_p, c_kr_p, s_a_p,
            b_k_s, b_v_s, b_idx_s, c_kv_s, c_kr_s, s_a_s)
```

```python
import functools
import math

import jax
import jax.numpy as jnp
import numpy as np
from jax import lax
from jax.experimental import pallas as pl
from jax.experimental.pallas import tpu as pltpu

D_MODEL = 1024
DEPTH = 4
PAGE_SIZE = 128
A_HEADS = 4
A_DK = 128
A_DV = 128
A_CHUNK = 32
B_HEADS = 4
B_KV_HEADS = 2
B_HEAD_DIM = 64
IDX_HEADS = 8
IDX_DIM = 64
TOPK_MAX = 256
C_HEADS = 4
C_NOPE = 64
C_ROPE = 32
C_V = 64
C_Q_LORA = 256
C_KV_LORA = 128
ROPE_THETA = 10000.0
Q_BLOCK = 128
EPS = 1e-5
MASK_VALUE = -1e30
DEEPNORM_ALPHA = (2 * DEPTH) ** 0.25
IN_SPLITS = (A_HEADS * A_DK, A_HEADS * A_DK, A_HEADS * A_DV, A_HEADS * A_DV,
             B_HEADS * B_HEAD_DIM, B_KV_HEADS * B_HEAD_DIM, B_KV_HEADS * B_HEAD_DIM,
             IDX_HEADS * IDX_DIM, IDX_DIM, IDX_HEADS,
             C_Q_LORA, C_KV_LORA, C_ROPE)
F32 = jnp.float32
BF16 = jnp.bfloat16


def _mm_kernel(x_ref, w_ref, o_ref):
    o_ref[...] = jnp.dot(x_ref[...].astype(BF16), w_ref[...].astype(BF16), preferred_element_type=F32)


def _matmul(x, w, tm=512, tn=512):
    m, k = x.shape
    n = w.shape[1]
    tm = min(tm, m)
    tn = min(tn, n)
    return pl.pallas_call(
        _mm_kernel,
        grid=(pl.cdiv(m, tm), pl.cdiv(n, tn)),
        in_specs=[pl.BlockSpec((tm, k), lambda i, j: (i, 0)),
                  pl.BlockSpec((k, tn), lambda i, j: (0, j))],
        out_specs=pl.BlockSpec((tm, tn), lambda i, j: (i, j)),
        out_shape=jax.ShapeDtypeStruct((m, n), F32),
        compiler_params=pltpu.CompilerParams(dimension_semantics=("parallel", "parallel"),
                                             vmem_limit_bytes=48 * 1024 * 1024),
        name="dense_matmul",
    )(x, w)


def _mm3(h, w):
    b, l, d = h.shape
    return _matmul(h.reshape(b * l, d), w).reshape(b, l, w.shape[1])


def layer_norm(x, g, b):
    mu = jnp.mean(x, -1, keepdims=True)
    var = jnp.mean(jnp.square(x - mu), -1, keepdims=True)
    return (x - mu) * lax.rsqrt(var + EPS) * g + b


def rms_norm(x, g):
    return x * lax.rsqrt(jnp.mean(jnp.square(x), -1, keepdims=True) + EPS) * g


def rope(x, pos):
    half = x.shape[-1] // 2
    inv = ROPE_THETA ** (-jnp.arange(half, dtype=F32) / half)
    ang = pos.astype(F32)[:, None] * inv[None, :]
    shape = (ang.shape[0],) + (1,) * (x.ndim - 3) + (half,)
    cos = jnp.cos(ang).reshape(shape)
    sin = jnp.sin(ang).reshape(shape)
    x1, x2 = x[..., :half], x[..., half:]
    return jnp.concatenate([x1 * cos - x2 * sin, x2 * cos + x1 * sin], -1)


def hgrn_lower_bounds(logits):
    p = jax.nn.softmax(logits.astype(F32), axis=0)
    return jnp.clip(jnp.cumsum(p, axis=0) - p[0:1], 0.0, 1.0 - 1e-4)


def to_blocks(t, qb):
    b, l = t.shape[:2]
    return t.reshape((b, l // qb, qb) + t.shape[2:]).swapaxes(0, 1)


def from_blocks(t):
    nb, b, qb = t.shape[:3]
    return t.swapaxes(0, 1).reshape((b, nb * qb) + t.shape[3:])


def project(h, w_in, pos, lb, cq_g, w_uq, ckv_g):
    b, l = h.shape[:2]
    offs = [int(o) for o in np.cumsum(IN_SPLITS)[:-1]]
    aq, af, ai, ag, bq, bk, bv, bqi, bki, bw, cdq, cdkv, ckr = jnp.split(_mm3(h, w_in), offs, axis=-1)
    z = af.reshape(b, l, A_HEADS, A_DK)
    lbh = lb.reshape(A_HEADS, A_DK)
    f = lbh + (1.0 - lbh) * jax.nn.sigmoid(z)
    a_logf = jnp.log(jnp.maximum(f, 1e-30))
    a_k = 1.0 - f
    a_q = aq.reshape(b, l, A_HEADS, A_DK)
    a_v = ai.reshape(b, l, A_HEADS, A_DV)
    b_q = rope(bq.reshape(b, l, B_HEADS, B_HEAD_DIM), pos)
    b_k = rope(bk.reshape(b, l, B_KV_HEADS, B_HEAD_DIM), pos)
    b_v = bv.reshape(b, l, B_KV_HEADS, B_HEAD_DIM)
    b_qi = rope(bqi.reshape(b, l, IDX_HEADS, IDX_DIM), pos)
    b_ki = rope(bki[:, :, None, :], pos)[:, :, 0]
    b_w = bw * (IDX_HEADS ** -0.5 * IDX_DIM ** -0.5)
    cq = jnp.einsum('blr,rhd->blhd', rms_norm(cdq, cq_g), w_uq)
    c_qn = cq[..., :C_NOPE]
    c_qr = rope(cq[..., C_NOPE:], pos)
    c_kv = rms_norm(cdkv, ckv_g)
    c_kr = rope(ckr[:, :, None, :], pos)[:, :, 0]
    return (a_q, a_logf, a_k, a_v, ag), (b_q, b_k, b_v, b_qi, b_ki, b_w), (c_qn, c_qr, c_kv, c_kr)


def hgrn2_recurrence(q, log_f, k, v, s0):
    b, l, h, _ = q.shape
    c = math.gcd(l, A_CHUNK)
    n = l // c

    def blocks(t):
        return t.reshape(b, n, c, h, t.shape[-1]).transpose(1, 0, 3, 2, 4)

    causal = jnp.tril(jnp.ones((c, c), bool))

    def step(s, inp):
        qc, lf, kc, vc = inp
        cum = jnp.cumsum(lf, axis=2)
        diff = cum[:, :, :, None, :] - cum[:, :, None, :, :]
        decay = jnp.exp(jnp.where(causal[:, :, None], diff, MASK_VALUE))
        attn = jnp.einsum('bhtd,bhtsd,bhsd->bhts', qc, decay, kc)
        o = attn @ vc + jnp.einsum('bhtd,bhdv->bhtv', qc * jnp.exp(cum), s)
        last = cum[:, :, -1:, :]
        s_new = jnp.exp(last)[:, :, 0, :, None] * s + jnp.einsum('bhsd,bhsv->bhdv', kc * jnp.exp(last - cum), vc)
        return s_new, o

    s_fin, o = lax.scan(step, s0, (blocks(q), blocks(log_f), blocks(k), blocks(v)))
    return o.transpose(1, 0, 3, 2, 4).reshape(b, l, h, -1), s_fin


def hgrn_output(o, g, norm_g):
    b, l = o.shape[:2]
    return rms_norm(o, norm_g).reshape(b, l, -1) * jax.nn.silu(g)


def indexer_scores(qi, w, ki, q_pos, k_pos):
    score = jnp.zeros(qi.shape[:2] + (ki.shape[1],), F32)
    for hh in range(IDX_HEADS):
        dots = jnp.einsum('btd,bsd->bts', qi[:, :, hh], ki)
        score = score + w[:, :, hh, None] * jax.nn.relu(dots)
    return jnp.where(k_pos[None, None, :] <= q_pos[None, :, None], score, MASK_VALUE)


def sparse_attention(q, kg, vg, valid):
    b, t = q.shape[:2]
    qg = q.reshape(b, t, B_KV_HEADS, B_HEADS // B_KV_HEADS, B_HEAD_DIM)
    s = jnp.einsum('btkgd,btjkd->btkgj', qg, kg) * (B_HEAD_DIM ** -0.5)
    s = jnp.where(valid[:, :, None, None, :], s, MASK_VALUE)
    p = jax.nn.softmax(s, axis=-1)
    return jnp.einsum('btkgj,btjkd->btkgd', p, vg).reshape(b, t, B_HEADS * B_HEAD_DIM)


def gather_rows(rows, idx):
    return jax.vmap(lambda r, i: r[i])(rows, idx)


def dsa_prompt(q, k, v, qi, ki, w, pos):
    l = q.shape[1]
    topk = min(TOPK_MAX, l // 4)
    qb = min(Q_BLOCK, l)

    def body(args):
        q_b, qi_b, w_b, qpos = args
        _, sel = lax.top_k(indexer_scores(qi_b, w_b, ki, qpos, pos), topk)
        return sparse_attention(q_b, gather_rows(k, sel), gather_rows(v, sel), sel <= qpos[None, :, None])

    out = lax.map(body, (to_blocks(q, qb), to_blocks(qi, qb), to_blocks(w, qb), pos.reshape(l // qb, qb)))
    return from_blocks(out)


def dsa_sample(q, k_new, v_new, qi, ki_new, w, q_pos, past, pool_k, pool_v, pool_idx, page_table):
    db, t = q.shape[:2]
    ki_all = jnp.concatenate([pool_idx[page_table].reshape(db, past, IDX_DIM), ki_new], 1)
    k_pos = jnp.arange(past + t, dtype=jnp.int32)
    topk = min(TOPK_MAX, (past + t) // 4)
    _, sel = lax.top_k(indexer_scores(qi, w, ki_all, q_pos, k_pos), topk)
    in_past = sel < past
    sel_p = jnp.minimum(sel, past - 1)
    phys_page = jnp.take_along_axis(page_table, (sel_p // PAGE_SIZE).reshape(db, -1), axis=1).reshape(sel.shape)
    phys = phys_page * PAGE_SIZE + sel_p % PAGE_SIZE
    sel_n = jnp.clip(sel - past, 0, t - 1)
    flat_k = pool_k.reshape((-1,) + pool_k.shape[2:])
    flat_v = pool_v.reshape((-1,) + pool_v.shape[2:])
    kg = jnp.where(in_past[..., None, None], flat_k[phys], gather_rows(k_new, sel_n))
    vg = jnp.where(in_past[..., None, None], flat_v[phys], gather_rows(v_new, sel_n))
    return sparse_attention(q, kg, vg, sel <= q_pos[None, :, None])


def mla_attend(q_abs, q_rope, c_kv, k_rope, q_pos, k_pos, w_uv):
    b, t = q_abs.shape[:2]
    s = jnp.einsum('bthr,bsr->bhts', q_abs, c_kv) + jnp.einsum('bthp,bsp->bhts', q_rope, k_rope)
    s = s * ((C_NOPE + C_ROPE) ** -0.5)
    s = jnp.where(k_pos[None, None, None, :] <= q_pos[None, None, :, None], s, MASK_VALUE)
    p = jax.nn.softmax(s, axis=-1)
    o_lat = jnp.einsum('bhts,bsr->bthr', p, c_kv)
    return jnp.einsum('bthr,rhv->bthv', o_lat, w_uv).reshape(b, t, C_HEADS * C_V)


def mla_prompt(q_abs, q_rope, c_kv, k_rope, pos, w_uv):
    l = q_abs.shape[1]
    qb = min(Q_BLOCK, l)

    def body(args):
        qa, qr, qp = args
        return mla_attend(qa, qr, c_kv, k_rope, qp, pos, w_uv)

    return from_blocks(lax.map(body, (to_blocks(q_abs, qb), to_blocks(q_rope, qb), pos.reshape(l // qb, qb))))


def prompt_mixers(h, pos, w_in, lb, a_norm_g, cq_g, w_uq, ckv_g, w_uk, w_uv):
    (a_q, a_logf, a_k, a_v, a_g), (b_q, b_k, b_v, b_qi, b_ki, b_w), (c_qn, c_qr, c_kv, c_kr) = project(
        h, w_in, pos, lb, cq_g, w_uq, ckv_g)
    s0 = jnp.zeros((h.shape[0], A_HEADS, A_DK, A_DV), F32)
    o_a, s_a = hgrn2_recurrence(a_q, a_logf, a_k, a_v, s0)
    o_a = hgrn_output(o_a, a_g, a_norm_g)
    o_b = dsa_prompt(b_q, b_k, b_v, b_qi, b_ki, b_w, pos)
    q_abs = jnp.einsum('bthn,rhn->bthr', c_qn, w_uk)
    o_c = mla_prompt(q_abs, c_qr, c_kv, c_kr, pos, w_uv)
    mix = jnp.concatenate([o_a, o_b, o_c], -1)
    return mix, (b_k, b_v, b_ki, c_kv, c_kr, s_a)


def sample_mixers(h, pos, past, pool_bk, pool_bv, pool_bidx, pool_ckv, pool_ckr, s_prev, page_table,
                  w_in, lb, a_norm_g, cq_g, w_uq, ckv_g, w_uk, w_uv):
    db = h.shape[0]
    (a_q, a_logf, a_k, a_v, a_g), (b_q, b_k, b_v, b_qi, b_ki, b_w), (c_qn, c_qr, c_kv, c_kr) = project(
        h, w_in, pos, lb, cq_g, w_uq, ckv_g)
    o_a, s_a = hgrn2_recurrence(a_q, a_logf, a_k, a_v, s_prev)
    o_a = hgrn_output(o_a, a_g, a_norm_g)
    o_b = dsa_sample(b_q, b_k, b_v, b_qi, b_ki, b_w, pos, past, pool_bk, pool_bv, pool_bidx, page_table)
    c_all = jnp.concatenate([pool_ckv[page_table].reshape(db, past, C_KV_LORA), c_kv], 1)
    kr_all = jnp.concatenate([pool_ckr[page_table].reshape(db, past, C_ROPE), c_kr], 1)
    q_abs = jnp.einsum('bthn,rhn->bthr', c_qn, w_uk)
    k_pos = jnp.arange(past + h.shape[1], dtype=jnp.int32)
    o_c = mla_attend(q_abs, c_qr, c_all, kr_all, pos, k_pos, w_uv)
    mix = jnp.concatenate([o_a, o_b, o_c], -1)
    return mix, (b_k, b_v, b_ki, c_kv, c_kr, s_a)


def finish_layer(x, mix, w_o, ln1_g, ln1_b, w_gate, w_up, w_down, ln2_g, ln2_b):
    x = layer_norm(DEEPNORM_ALPHA * x + _mm3(mix, w_o), ln1_g, ln1_b)
    ffn = _mm3(jax.nn.silu(_mm3(x, w_gate)) * _mm3(x, w_up), w_down)
    return layer_norm(DEEPNORM_ALPHA * x + ffn, ln2_g, ln2_b)


def kernel(x_prompt, x_sample, cache_b_k, cache_b_v, cache_b_idx, cache_c_kv, cache_c_kr, state_a,
           page_table, ln_in_g, ln_in_b, w_in, hgrn_lb_logits, hgrn_norm_g, mla_q_norm_g, mla_w_uq,
           mla_kv_norm_g, mla_w_uk, mla_w_uv, w_o, ln1_g, ln1_b, w_gate, w_up, w_down, ln2_g, ln2_b):
    lower_bounds = hgrn_lower_bounds(hgrn_lb_logits)
    past = page_table.shape[1] * PAGE_SIZE
    pos_p = jnp.arange(x_prompt.shape[1], dtype=jnp.int32)
    pos_s = past + jnp.arange(x_sample.shape[1], dtype=jnp.int32)
    xp = layer_norm(x_prompt, ln_in_g, ln_in_b)
    xs = layer_norm(x_sample, ln_in_g, ln_in_b)
    new_p = ([], [], [], [], [], [])
    new_s = ([], [], [], [], [], [])
    for l in range(DEPTH):
        mix_w = (w_in[l], lower_bounds[l], hgrn_norm_g[l], mla_q_norm_g[l], mla_w_uq[l],
                 mla_kv_norm_g[l], mla_w_uk[l], mla_w_uv[l])
        ffn_w = (w_o[l], ln1_g[l], ln1_b[l], w_gate[l], w_up[l], w_down[l], ln2_g[l], ln2_b[l])
        mix_p, st_p = prompt_mixers(xp, pos_p, *mix_w)
        mix_s, st_s = sample_mixers(xs, pos_s, past, cache_b_k[l], cache_b_v[l], cache_b_idx[l],
                                    cache_c_kv[l], cache_c_kr[l], state_a[l], page_table, *mix_w)
        xp = finish_layer(xp, mix_p, *ffn_w)
        xs = finish_layer(xs, mix_s, *ffn_w)
        for lst, st in zip(new_p, st_p):
            lst.append(st)
        for lst, st in zip(new_s, st_s):
            lst.append(st)
    b_k_p, b_v_p, b_idx_p, c_kv_p, c_kr_p, s_a_p = [jnp.stack(v, 0) for v in new_p]
    b_k_s, b_v_s, b_idx_s, c_kv_s, c_kr_s, s_a_s = [jnp.stack(v, 0) for v in new_s]
    return (xp, xs, b_k_p, b_v_p, b_idx_p, c_kv_p, c_kr_p, s_a_p,
            b_k_s, b_v_s, b_idx_s, c_kv_s, c_kr_s, s_a_s)
```

```python
import functools
import math

import jax
import jax.numpy as jnp
import numpy as np
from jax import lax
from jax.experimental import pallas as pl
from jax.experimental.pallas import tpu as pltpu

D_MODEL = 1024
DEPTH = 4
PAGE_SIZE = 128
A_HEADS = 4
A_DK = 128
A_DV = 128
A_CHUNK = 32
B_HEADS = 4
B_KV_HEADS = 2
B_HEAD_DIM = 64
IDX_HEADS = 8
IDX_DIM = 64
TOPK_MAX = 256
C_HEADS = 4
C_NOPE = 64
C_ROPE = 32
C_V = 64
C_Q_LORA = 256
C_KV_LORA = 128
ROPE_THETA = 10000.0
Q_BLOCK = 128
EPS = 1e-5
MASK_VALUE = -1e30
DEEPNORM_ALPHA = (2 * DEPTH) ** 0.25
IN_SPLITS = (A_HEADS * A_DK, A_HEADS * A_DK, A_HEADS * A_DV, A_HEADS * A_DV,
             B_HEADS * B_HEAD_DIM, B_KV_HEADS * B_HEAD_DIM, B_KV_HEADS * B_HEAD_DIM,
             IDX_HEADS * IDX_DIM, IDX_DIM, IDX_HEADS,
             C_Q_LORA, C_KV_LORA, C_ROPE)
F32 = jnp.float32
BF16 = jnp.bfloat16


def _mm_kernel(x_ref, w_ref, o_ref):
    o_ref[...] = jnp.dot(x_ref[...].astype(BF16), w_ref[...].astype(BF16), preferred_element_type=F32)


def _matmul(x, w, tm=512, tn=512):
    m, k = x.shape
    n = w.shape[1]
    tm = min(tm, m)
    tn = min(tn, n)
    return pl.pallas_call(
        _mm_kernel,
        grid=(pl.cdiv(m, tm), pl.cdiv(n, tn)),
        in_specs=[pl.BlockSpec((tm, k), lambda i, j: (i, 0)),
                  pl.BlockSpec((k, tn), lambda i, j: (0, j))],
        out_specs=pl.BlockSpec((tm, tn), lambda i, j: (i, j)),
        out_shape=jax.ShapeDtypeStruct((m, n), F32),
        compiler_params=pltpu.CompilerParams(dimension_semantics=("parallel", "parallel"),
                                             vmem_limit_bytes=48 * 1024 * 1024),
        name="dense_matmul",
    )(x, w)


def _mm3(h, w):
    b, l, d = h.shape
    return _matmul(h.reshape(b * l, d), w).reshape(b, l, w.shape[1])


def layer_norm(x, g, b):
    mu = jnp.mean(x, -1, keepdims=True)
    var = jnp.mean(jnp.square(x - mu), -1, keepdims=True)
    return (x - mu) * lax.rsqrt(var + EPS) * g + b


def rms_norm(x, g):
    return x * lax.rsqrt(jnp.mean(jnp.square(x), -1, keepdims=True) + EPS) * g


def rope(x, pos):
    half = x.shape[-1] // 2
    inv = ROPE_THETA ** (-jnp.arange(half, dtype=F32) / half)
    ang = pos.astype(F32)[:, None] * inv[None, :]
    shape = (ang.shape[0],) + (1,) * (x.ndim - 3) + (half,)
    cos = jnp.cos(ang).reshape(shape)
    sin = jnp.sin(ang).reshape(shape)
    x1, x2 = x[..., :half], x[..., half:]
    return jnp.concatenate([x1 * cos - x2 * sin, x2 * cos + x1 * sin], -1)


def hgrn_lower_bounds(logits):
    p = jax.nn.softmax(logits.astype(F32), axis=0)
    return jnp.clip(jnp.cumsum(p, axis=0) - p[0:1], 0.0, 1.0 - 1e-4)


def to_blocks(t, qb):
    b, l = t.shape[:2]
    return t.reshape((b, l // qb, qb) + t.shape[2:]).swapaxes(0, 1)


def from_blocks(t):
    nb, b, qb = t.shape[:3]
    return t.swapaxes(0, 1).reshape((b, nb * qb) + t.shape[3:])


def project(h, w_in, pos, lb, cq_g, w_uq, ckv_g):
    b, l = h.shape[:2]
    offs = [int(o) for o in np.cumsum(IN_SPLITS)[:-1]]
    aq, af, ai, ag, bq, bk, bv, bqi, bki, bw, cdq, cdkv, ckr = jnp.split(_mm3(h, w_in), offs, axis=-1)
    z = af.reshape(b, l, A_HEADS, A_DK)
    lbh = lb.reshape(A_HEADS, A_DK)
    f = lbh + (1.0 - lbh) * jax.nn.sigmoid(z)
    a_logf = jnp.log(jnp.maximum(f, 1e-30))
    a_k = 1.0 - f
    a_q = aq.reshape(b, l, A_HEADS, A_DK)
    a_v = ai.reshape(b, l, A_HEADS, A_DV)
    b_q = rope(bq.reshape(b, l, B_HEADS, B_HEAD_DIM), pos)
    b_k = rope(bk.reshape(b, l, B_KV_HEADS, B_HEAD_DIM), pos)
    b_v = bv.reshape(b, l, B_KV_HEADS, B_HEAD_DIM)
    b_qi = rope(bqi.reshape(b, l, IDX_HEADS, IDX_DIM), pos)
    b_ki = rope(bki[:, :, None, :], pos)[:, :, 0]
    b_w = bw * (IDX_HEADS ** -0.5 * IDX_DIM ** -0.5)
    cq = jnp.einsum('blr,rhd->blhd', rms_norm(cdq, cq_g), w_uq)
    c_qn = cq[..., :C_NOPE]
    c_qr = rope(cq[..., C_NOPE:], pos)
    c_kv = rms_norm(cdkv, ckv_g)
    c_kr = rope(ckr[:, :, None, :], pos)[:, :, 0]
    return (a_q, a_logf, a_k, a_v, ag), (b_q, b_k, b_v, b_qi, b_ki, b_w), (c_qn, c_qr, c_kv, c_kr)


def hgrn2_recurrence(q, log_f, k, v, s0):
    b, l, h, _ = q.shape
    c = math.gcd(l, A_CHUNK)
    n = l // c

    def blocks(t):
        return t.reshape(b, n, c, h, t.shape[-1]).transpose(1, 0, 3, 2, 4)

    causal = jnp.tril(jnp.ones((c, c), bool))

    def step(s, inp):
        qc, lf, kc, vc = inp
        cum = jnp.cumsum(lf, axis=2)
        diff = cum[:, :, :, None, :] - cum[:, :, None, :, :]
        decay = jnp.exp(jnp.where(causal[:, :, None], diff, MASK_VALUE))
        attn = jnp.einsum('bhtd,bhtsd,bhsd->bhts', qc, decay, kc)
        o = attn @ vc + jnp.einsum('bhtd,bhdv->bhtv', qc * jnp.exp(cum), s)
        last = cum[:, :, -1:, :]
        s_new = jnp.exp(last)[:, :, 0, :, None] * s + jnp.einsum('bhsd,bhsv->bhdv', kc * jnp.exp(last - cum), vc)
        return s_new, o

    s_fin, o = lax.scan(step, s0, (blocks(q), blocks(log_f), blocks(k), blocks(v)))
    return o.transpose(1, 0, 3, 2, 4).reshape(b, l, h, -1), s_fin


def hgrn_output(o, g, norm_g):
    b, l = o.shape[:2]
    return rms_norm(o, norm_g).reshape(b, l, -1) * jax.nn.silu(g)


def indexer_scores(qi, w, ki, q_pos, k_pos):
    score = jnp.zeros(qi.shape[:2] + (ki.shape[1],), F32)
    for hh in range(IDX_HEADS):
        dots = jnp.einsum('btd,bsd->bts', qi[:, :, hh], ki)
        score = score + w[:, :, hh, None] * jax.nn.relu(dots)
    return jnp.where(k_pos[None, None, :] <= q_pos[None, :, None], score, MASK_VALUE)


def sparse_attention(q, kg, vg, valid):
    b, t = q.shape[:2]
    qg = q.reshape(b, t, B_KV_HEADS, B_HEADS // B_KV_HEADS, B_HEAD_DIM)
    s = jnp.einsum('btkgd,btjkd->btkgj', qg, kg) * (B_HEAD_DIM ** -0.5)
    s = jnp.where(valid[:, :, None, None, :], s, MASK_VALUE)
    p = jax.nn.softmax(s, axis=-1)
    return jnp.einsum('btkgj,btjkd->btkgd', p, vg).reshape(b, t, B_HEADS * B_HEAD_DIM)


def gather_rows(rows, idx):
    return jax.vmap(lambda r, i: r[i])(rows, idx)


INT_MIN = -2 ** 31
_NT = (((1,), (1,)), ((), ()))


def _dsa_prompt_kernel(qi_ref, wT_ref, ki_ref, q_ref, k_ref, vT_ref, o_ref, key_ref, sel_ref, *, topk, scale):
    n_idx, tq = qi_ref.shape[1], qi_ref.shape[2]
    L = ki_ref.shape[1]
    qb = pl.program_id(1)
    ki = ki_ref[0].astype(BF16)
    acc = jnp.zeros((L, tq), F32)
    for h in range(n_idx):
        d = lax.dot_general(ki, qi_ref[0, h].astype(BF16), _NT, preferred_element_type=F32)
        acc = acc + wT_ref[0, h:h + 1, :] * jnp.maximum(d, 0.0)
    kpos = lax.broadcasted_iota(jnp.int32, (L, tq), 0)
    qpos = qb * tq + lax.broadcasted_iota(jnp.int32, (L, tq), 1)
    causal = kpos <= qpos
    s = jnp.where(causal, acc, MASK_VALUE)
    s = jnp.where(s == 0.0, 0.0, s)
    bits = pltpu.bitcast(s, jnp.int32)
    key_ref[...] = bits ^ ((bits >> 31) & 0x7FFFFFFF)

    def count_ge(cand):
        return jnp.sum(jnp.where(key_ref[...] >= cand, 1.0, 0.0), axis=0, keepdims=True)

    kf = float(topk)
    t0 = jnp.where(count_ge(jnp.zeros((1, tq), jnp.int32)) >= kf, 0, INT_MIN).astype(jnp.int32)

    def step(i, t):
        cand = t | jnp.left_shift(jnp.int32(1), 30 - i)
        return jnp.where(count_ge(cand) >= kf, cand, t)

    thr = lax.fori_loop(0, 31, step, t0)
    need = kf - jnp.sum(jnp.where(key_ref[...] > thr, 1.0, 0.0), axis=0, keepdims=True)
    tri = (lax.broadcasted_iota(jnp.int32, (128, 128), 0) >= lax.broadcasted_iota(jnp.int32, (128, 128), 1))
    tri = jnp.where(tri, 1.0, 0.0).astype(BF16)
    offset = jnp.zeros((1, tq), F32)
    for c in range(L // 128):
        key_c = key_ref[c * 128:(c + 1) * 128, :]
        tie = key_c == thr
        pref = jnp.dot(tri, jnp.where(tie, 1.0, 0.0).astype(BF16), preferred_element_type=F32) + offset
        offset = pref[127:128, :]
        chosen = jnp.where(key_c > thr, 1.0, jnp.where(tie, jnp.where(pref <= need, 1.0, 0.0), 0.0))
        sel_ref[c * 128:(c + 1) * 128, :] = jnp.where(causal[c * 128:(c + 1) * 128, :], chosen, 0.0)

    n_kv = k_ref.shape[1]
    grp = q_ref.shape[1] // n_kv
    for j in range(n_kv):
        kj = k_ref[0, j].astype(BF16)
        vTj = vT_ref[0, j].astype(BF16)
        for g in range(grp):
            h = j * grp + g
            sT = lax.dot_general(kj, q_ref[0, h].astype(BF16), _NT, preferred_element_type=F32) * scale
            sT = jnp.where(sel_ref[...] > 0.0, sT, MASK_VALUE)
            m = jnp.max(sT, axis=0, keepdims=True)
            p = jnp.exp(sT - m)
            l = jnp.sum(p, axis=0, keepdims=True)
            oT = jnp.dot(vTj, p.astype(BF16), preferred_element_type=F32)
            o_ref[0, h] = oT / l


def dsa_prompt(q, k, v, qi, ki, w, pos, tq=128):
    b, l, nh, dh = q.shape
    n_idx, n_kv = qi.shape[2], k.shape[2]
    topk = min(TOPK_MAX, l // 4)
    out = pl.pallas_call(
        functools.partial(_dsa_prompt_kernel, topk=topk, scale=dh ** -0.5),
        grid=(b, l // tq),
        in_specs=[
            pl.BlockSpec((1, n_idx, tq, qi.shape[3]), lambda bi, qb: (bi, 0, qb, 0)),
            pl.BlockSpec((1, n_idx, tq), lambda bi, qb: (bi, 0, qb)),
            pl.BlockSpec((1, l, ki.shape[2]), lambda bi, qb: (bi, 0, 0)),
            pl.BlockSpec((1, nh, tq, dh), lambda bi, qb: (bi, 0, qb, 0)),
            pl.BlockSpec((1, n_kv, l, dh), lambda bi, qb: (bi, 0, 0, 0)),
            pl.BlockSpec((1, n_kv, dh, l), lambda bi, qb: (bi, 0, 0, 0)),
        ],
        out_specs=pl.BlockSpec((1, nh, dh, tq), lambda bi, qb: (bi, 0, 0, qb)),
        out_shape=jax.ShapeDtypeStruct((b, nh, dh, l), F32),
        scratch_shapes=[pltpu.VMEM((l, tq), jnp.int32), pltpu.VMEM((l, tq), F32)],
        compiler_params=pltpu.CompilerParams(dimension_semantics=("parallel", "arbitrary"),
                                             vmem_limit_bytes=48 * 1024 * 1024),
        name="dsa_prompt",
    )(qi.transpose(0, 2, 1, 3), w.transpose(0, 2, 1), ki, q.transpose(0, 2, 1, 3),
      k.transpose(0, 2, 1, 3), v.transpose(0, 2, 3, 1))
    return out.transpose(0, 3, 1, 2).reshape(b, l, nh * dh)


def dsa_sample(q, k_new, v_new, qi, ki_new, w, q_pos, past, pool_k, pool_v, pool_idx, page_table):
    db, t = q.shape[:2]
    ki_all = jnp.concatenate([pool_idx[page_table].reshape(db, past, IDX_DIM), ki_new], 1)
    k_pos = jnp.arange(past + t, dtype=jnp.int32)
    topk = min(TOPK_MAX, (past + t) // 4)
    _, sel = lax.top_k(indexer_scores(qi, w, ki_all, q_pos, k_pos), topk)
    in_past = sel < past
    sel_p = jnp.minimum(sel, past - 1)
    phys_page = jnp.take_along_axis(page_table, (sel_p // PAGE_SIZE).reshape(db, -1), axis=1).reshape(sel.shape)
    phys = phys_page * PAGE_SIZE + sel_p % PAGE_SIZE
    sel_n = jnp.clip(sel - past, 0, t - 1)
    flat_k = pool_k.reshape((-1,) + pool_k.shape[2:])
    flat_v = pool_v.reshape((-1,) + pool_v.shape[2:])
    kg = jnp.where(in_past[..., None, None], flat_k[phys], gather_rows(k_new, sel_n))
    vg = jnp.where(in_past[..., None, None], flat_v[phys], gather_rows(v_new, sel_n))
    return sparse_attention(q, kg, vg, sel <= q_pos[None, :, None])


def mla_attend(q_abs, q_rope, c_kv, k_rope, q_pos, k_pos, w_uv):
    b, t = q_abs.shape[:2]
    s = jnp.einsum('bthr,bsr->bhts', q_abs, c_kv) + jnp.einsum('bthp,bsp->bhts', q_rope, k_rope)
    s = s * ((C_NOPE + C_ROPE) ** -0.5)
    s = jnp.where(k_pos[None, None, None, :] <= q_pos[None, None, :, None], s, MASK_VALUE)
    p = jax.nn.softmax(s, axis=-1)
    o_lat = jnp.einsum('bhts,bsr->bthr', p, c_kv)
    return jnp.einsum('bthr,rhv->bthv', o_lat, w_uv).reshape(b, t, C_HEADS * C_V)


def mla_prompt(q_abs, q_rope, c_kv, k_rope, pos, w_uv):
    l = q_abs.shape[1]
    qb = min(Q_BLOCK, l)

    def body(args):
        qa, qr, qp = args
        return mla_attend(qa, qr, c_kv, k_rope, qp, pos, w_uv)

    return from_blocks(lax.map(body, (to_blocks(q_abs, qb), to_blocks(q_rope, qb), pos.reshape(l // qb, qb))))


def prompt_mixers(h, pos, w_in, lb, a_norm_g, cq_g, w_uq, ckv_g, w_uk, w_uv):
    (a_q, a_logf, a_k, a_v, a_g), (b_q, b_k, b_v, b_qi, b_ki, b_w), (c_qn, c_qr, c_kv, c_kr) = project(
        h, w_in, pos, lb, cq_g, w_uq, ckv_g)
    s0 = jnp.zeros((h.shape[0], A_HEADS, A_DK, A_DV), F32)
    o_a, s_a = hgrn2_recurrence(a_q, a_logf, a_k, a_v, s0)
    o_a = hgrn_output(o_a, a_g, a_norm_g)
    o_b = dsa_prompt(b_q, b_k, b_v, b_qi, b_ki, b_w, pos)
    q_abs = jnp.einsum('bthn,rhn->bthr', c_qn, w_uk)
    o_c = mla_prompt(q_abs, c_qr, c_kv, c_kr, pos, w_uv)
    mix = jnp.concatenate([o_a, o_b, o_c], -1)
    return mix, (b_k, b_v, b_ki, c_kv, c_kr, s_a)


def sample_mixers(h, pos, past, pool_bk, pool_bv, pool_bidx, pool_ckv, pool_ckr, s_prev, page_table,
                  w_in, lb, a_norm_g, cq_g, w_uq, ckv_g, w_uk, w_uv):
    db = h.shape[0]
    (a_q, a_logf, a_k, a_v, a_g), (b_q, b_k, b_v, b_qi, b_ki, b_w), (c_qn, c_qr, c_kv, c_kr) = project(
        h, w_in, pos, lb, cq_g, w_uq, ckv_g)
    o_a, s_a = hgrn2_recurrence(a_q, a_logf, a_k, a_v, s_prev)
    o_a = hgrn_output(o_a, a_g, a_norm_g)
    o_b = dsa_sample(b_q, b_k, b_v, b_qi, b_ki, b_w, pos, past, pool_bk, pool_bv, pool_bidx, page_table)
    c_all = jnp.concatenate([pool_ckv[page_table].reshape(db, past, C_KV_LORA), c_kv], 1)
    kr_all = jnp.concatenate([pool_ckr[page_table].reshape(db, past, C_ROPE), c_kr], 1)
    q_abs = jnp.einsum('bthn,rhn->bthr', c_qn, w_uk)
    k_pos = jnp.arange(past + h.shape[1], dtype=jnp.int32)
    o_c = mla_attend(q_abs, c_qr, c_all, kr_all, pos, k_pos, w_uv)
    mix = jnp.concatenate([o_a, o_b, o_c], -1)
    return mix, (b_k, b_v, b_ki, c_kv, c_kr, s_a)


def finish_layer(x, mix, w_o, ln1_g, ln1_b, w_gate, w_up, w_down, ln2_g, ln2_b):
    x = layer_norm(DEEPNORM_ALPHA * x + _mm3(mix, w_o), ln1_g, ln1_b)
    ffn = _mm3(jax.nn.silu(_mm3(x, w_gate)) * _mm3(x, w_up), w_down)
    return layer_norm(DEEPNORM_ALPHA * x + ffn, ln2_g, ln2_b)


def kernel(x_prompt, x_sample, cache_b_k, cache_b_v, cache_b_idx, cache_c_kv, cache_c_kr, state_a,
           page_table, ln_in_g, ln_in_b, w_in, hgrn_lb_logits, hgrn_norm_g, mla_q_norm_g, mla_w_uq,
           mla_kv_norm_g, mla_w_uk, mla_w_uv, w_o, ln1_g, ln1_b, w_gate, w_up, w_down, ln2_g, ln2_b):
    lower_bounds = hgrn_lower_bounds(hgrn_lb_logits)
    past = page_table.shape[1] * PAGE_SIZE
    pos_p = jnp.arange(x_prompt.shape[1], dtype=jnp.int32)
    pos_s = past + jnp.arange(x_sample.shape[1], dtype=jnp.int32)
    xp = layer_norm(x_prompt, ln_in_g, ln_in_b)
    xs = layer_norm(x_sample, ln_in_g, ln_in_b)
    new_p = ([], [], [], [], [], [])
    new_s = ([], [], [], [], [], [])
    for l in range(DEPTH):
        mix_w = (w_in[l], lower_bounds[l], hgrn_norm_g[l], mla_q_norm_g[l], mla_w_uq[l],
                 mla_kv_norm_g[l], mla_w_uk[l], mla_w_uv[l])
        ffn_w = (w_o[l], ln1_g[l], ln1_b[l], w_gate[l], w_up[l], w_down[l], ln2_g[l], ln2_b[l])
        mix_p, st_p = prompt_mixers(xp, pos_p, *mix_w)
        mix_s, st_s = sample_mixers(xs, pos_s, past, cache_b_k[l], cache_b_v[l], cache_b_idx[l],
                                    cache_c_kv[l], cache_c_kr[l], state_a[l], page_table, *mix_w)
        xp = finish_layer(xp, mix_p, *ffn_w)
        xs = finish_layer(xs, mix_s, *ffn_w)
        for lst, st in zip(new_p, st_p):
            lst.append(st)
        for lst, st in zip(new_s, st_s):
            lst.append(st)
    b_k_p, b_v_p, b_idx_p, c_kv_p, c_kr_p, s_a_p = [jnp.stack(v, 0) for v in new_p]
    b_k_s, b_v_s, b_idx_s, c_kv_s, c_kr_s, s_a_s = [jnp.stack(v, 0) for v in new_s]
    return (xp, xs, b_k_p, b_v_p, b_idx_p, c_kv_p, c_kr_p, s_a_p,
            b_k_s, b_v_s, b_idx_s, c_kv_s, c_kr_s, s_a_s)
```

```python
import functools
import math

import jax
import jax.numpy as jnp
import numpy as np
from jax import lax
from jax.experimental import pallas as pl
from jax.experimental.pallas import tpu as pltpu

D_MODEL = 1024
DEPTH = 4
PAGE_SIZE = 128
A_HEADS = 4
A_DK = 128
A_DV = 128
A_CHUNK = 32
B_HEADS = 4
B_KV_HEADS = 2
B_HEAD_DIM = 64
IDX_HEADS = 8
IDX_DIM = 64
TOPK_MAX = 256
C_HEADS = 4
C_NOPE = 64
C_ROPE = 32
C_V = 64
C_Q_LORA = 256
C_KV_LORA = 128
ROPE_THETA = 10000.0
Q_BLOCK = 128
EPS = 1e-5
MASK_VALUE = -1e30
DEEPNORM_ALPHA = (2 * DEPTH) ** 0.25
IN_SPLITS = (A_HEADS * A_DK, A_HEADS * A_DK, A_HEADS * A_DV, A_HEADS * A_DV,
             B_HEADS * B_HEAD_DIM, B_KV_HEADS * B_HEAD_DIM, B_KV_HEADS * B_HEAD_DIM,
             IDX_HEADS * IDX_DIM, IDX_DIM, IDX_HEADS,
             C_Q_LORA, C_KV_LORA, C_ROPE)
F32 = jnp.float32
BF16 = jnp.bfloat16


def _mm_kernel(x_ref, w_ref, o_ref):
    o_ref[...] = jnp.dot(x_ref[...].astype(BF16), w_ref[...].astype(BF16), preferred_element_type=F32)


def _matmul(x, w, tm=512, tn=512):
    m, k = x.shape
    n = w.shape[1]
    tm = min(tm, m)
    tn = min(tn, n)
    return pl.pallas_call(
        _mm_kernel,
        grid=(pl.cdiv(m, tm), pl.cdiv(n, tn)),
        in_specs=[pl.BlockSpec((tm, k), lambda i, j: (i, 0)),
                  pl.BlockSpec((k, tn), lambda i, j: (0, j))],
        out_specs=pl.BlockSpec((tm, tn), lambda i, j: (i, j)),
        out_shape=jax.ShapeDtypeStruct((m, n), F32),
        compiler_params=pltpu.CompilerParams(dimension_semantics=("parallel", "parallel"),
                                             vmem_limit_bytes=48 * 1024 * 1024),
        name="dense_matmul",
    )(x, w)


def _mm3(h, w):
    b, l, d = h.shape
    return _matmul(h.reshape(b * l, d), w).reshape(b, l, w.shape[1])


def layer_norm(x, g, b):
    mu = jnp.mean(x, -1, keepdims=True)
    var = jnp.mean(jnp.square(x - mu), -1, keepdims=True)
    return (x - mu) * lax.rsqrt(var + EPS) * g + b


def rms_norm(x, g):
    return x * lax.rsqrt(jnp.mean(jnp.square(x), -1, keepdims=True) + EPS) * g


def rope(x, pos):
    half = x.shape[-1] // 2
    inv = ROPE_THETA ** (-jnp.arange(half, dtype=F32) / half)
    ang = pos.astype(F32)[:, None] * inv[None, :]
    shape = (ang.shape[0],) + (1,) * (x.ndim - 3) + (half,)
    cos = jnp.cos(ang).reshape(shape)
    sin = jnp.sin(ang).reshape(shape)
    x1, x2 = x[..., :half], x[..., half:]
    return jnp.concatenate([x1 * cos - x2 * sin, x2 * cos + x1 * sin], -1)


def hgrn_lower_bounds(logits):
    p = jax.nn.softmax(logits.astype(F32), axis=0)
    return jnp.clip(jnp.cumsum(p, axis=0) - p[0:1], 0.0, 1.0 - 1e-4)


def to_blocks(t, qb):
    b, l = t.shape[:2]
    return t.reshape((b, l // qb, qb) + t.shape[2:]).swapaxes(0, 1)


def from_blocks(t):
    nb, b, qb = t.shape[:3]
    return t.swapaxes(0, 1).reshape((b, nb * qb) + t.shape[3:])


def project(h, w_in, pos, lb, cq_g, w_uq, ckv_g):
    b, l = h.shape[:2]
    offs = [int(o) for o in np.cumsum(IN_SPLITS)[:-1]]
    aq, af, ai, ag, bq, bk, bv, bqi, bki, bw, cdq, cdkv, ckr = jnp.split(_mm3(h, w_in), offs, axis=-1)
    z = af.reshape(b, l, A_HEADS, A_DK)
    lbh = lb.reshape(A_HEADS, A_DK)
    f = lbh + (1.0 - lbh) * jax.nn.sigmoid(z)
    a_logf = jnp.log(jnp.maximum(f, 1e-30))
    a_k = 1.0 - f
    a_q = aq.reshape(b, l, A_HEADS, A_DK)
    a_v = ai.reshape(b, l, A_HEADS, A_DV)
    b_q = rope(bq.reshape(b, l, B_HEADS, B_HEAD_DIM), pos)
    b_k = rope(bk.reshape(b, l, B_KV_HEADS, B_HEAD_DIM), pos)
    b_v = bv.reshape(b, l, B_KV_HEADS, B_HEAD_DIM)
    b_qi = rope(bqi.reshape(b, l, IDX_HEADS, IDX_DIM), pos)
    b_ki = rope(bki[:, :, None, :], pos)[:, :, 0]
    b_w = bw * (IDX_HEADS ** -0.5 * IDX_DIM ** -0.5)
    cq = jnp.einsum('blr,rhd->blhd', rms_norm(cdq, cq_g), w_uq)
    c_qn = cq[..., :C_NOPE]
    c_qr = rope(cq[..., C_NOPE:], pos)
    c_kv = rms_norm(cdkv, ckv_g)
    c_kr = rope(ckr[:, :, None, :], pos)[:, :, 0]
    return (a_q, a_logf, a_k, a_v, ag), (b_q, b_k, b_v, b_qi, b_ki, b_w), (c_qn, c_qr, c_kv, c_kr)


def hgrn2_recurrence(q, log_f, k, v, s0):
    b, l, h, _ = q.shape
    c = math.gcd(l, A_CHUNK)
    n = l // c

    def blocks(t):
        return t.reshape(b, n, c, h, t.shape[-1]).transpose(1, 0, 3, 2, 4)

    causal = jnp.tril(jnp.ones((c, c), bool))

    def step(s, inp):
        qc, lf, kc, vc = inp
        cum = jnp.cumsum(lf, axis=2)
        diff = cum[:, :, :, None, :] - cum[:, :, None, :, :]
        decay = jnp.exp(jnp.where(causal[:, :, None], diff, MASK_VALUE))
        attn = jnp.einsum('bhtd,bhtsd,bhsd->bhts', qc, decay, kc)
        o = attn @ vc + jnp.einsum('bhtd,bhdv->bhtv', qc * jnp.exp(cum), s)
        last = cum[:, :, -1:, :]
        s_new = jnp.exp(last)[:, :, 0, :, None] * s + jnp.einsum('bhsd,bhsv->bhdv', kc * jnp.exp(last - cum), vc)
        return s_new, o

    s_fin, o = lax.scan(step, s0, (blocks(q), blocks(log_f), blocks(k), blocks(v)))
    return o.transpose(1, 0, 3, 2, 4).reshape(b, l, h, -1), s_fin


def hgrn_output(o, g, norm_g):
    b, l = o.shape[:2]
    return rms_norm(o, norm_g).reshape(b, l, -1) * jax.nn.silu(g)


def indexer_scores(qi, w, ki, q_pos, k_pos):
    score = jnp.zeros(qi.shape[:2] + (ki.shape[1],), F32)
    for hh in range(IDX_HEADS):
        dots = jnp.einsum('btd,bsd->bts', qi[:, :, hh], ki)
        score = score + w[:, :, hh, None] * jax.nn.relu(dots)
    return jnp.where(k_pos[None, None, :] <= q_pos[None, :, None], score, MASK_VALUE)


def sparse_attention(q, kg, vg, valid):
    b, t = q.shape[:2]
    qg = q.reshape(b, t, B_KV_HEADS, B_HEADS // B_KV_HEADS, B_HEAD_DIM)
    s = jnp.einsum('btkgd,btjkd->btkgj', qg, kg) * (B_HEAD_DIM ** -0.5)
    s = jnp.where(valid[:, :, None, None, :], s, MASK_VALUE)
    p = jax.nn.softmax(s, axis=-1)
    return jnp.einsum('btkgj,btjkd->btkgd', p, vg).reshape(b, t, B_HEADS * B_HEAD_DIM)


def gather_rows(rows, idx):
    return jax.vmap(lambda r, i: r[i])(rows, idx)


INT_MIN = -2 ** 31
_NT = (((1,), (1,)), ((), ()))


def _dsa_prompt_kernel(qi_ref, wT_ref, ki_ref, q_ref, k_ref, vT_ref, o_ref, key_ref, sel_ref, *, topk, scale):
    n_idx, tq = qi_ref.shape[1], qi_ref.shape[2]
    L = ki_ref.shape[1]
    qb = pl.program_id(1)
    ki = ki_ref[0].astype(BF16)
    acc = jnp.zeros((L, tq), F32)
    for h in range(n_idx):
        d = lax.dot_general(ki, qi_ref[0, h].astype(BF16), _NT, preferred_element_type=F32)
        acc = acc + wT_ref[0, h:h + 1, :] * jnp.maximum(d, 0.0)
    kpos = lax.broadcasted_iota(jnp.int32, (L, tq), 0)
    qpos = qb * tq + lax.broadcasted_iota(jnp.int32, (L, tq), 1)
    causal = kpos <= qpos
    s = jnp.where(causal, acc, MASK_VALUE)
    s = jnp.where(s == 0.0, 0.0, s)
    bits = pltpu.bitcast(s, jnp.int32)
    key_ref[...] = bits ^ ((bits >> 31) & 0x7FFFFFFF)

    def count_ge(cand):
        return jnp.sum(jnp.where(key_ref[...] >= cand, 1.0, 0.0), axis=0, keepdims=True)

    kf = float(topk)
    t0 = jnp.where(count_ge(jnp.zeros((1, tq), jnp.int32)) >= kf, 0, INT_MIN).astype(jnp.int32)

    def step(i, t):
        cand = t | jnp.left_shift(jnp.int32(1), 30 - i)
        return jnp.where(count_ge(cand) >= kf, cand, t)

    thr = lax.fori_loop(0, 31, step, t0)
    need = kf - jnp.sum(jnp.where(key_ref[...] > thr, 1.0, 0.0), axis=0, keepdims=True)
    tri = (lax.broadcasted_iota(jnp.int32, (128, 128), 0) >= lax.broadcasted_iota(jnp.int32, (128, 128), 1))
    tri = jnp.where(tri, 1.0, 0.0).astype(BF16)
    offset = jnp.zeros((1, tq), F32)
    for c in range(L // 128):
        key_c = key_ref[c * 128:(c + 1) * 128, :]
        tie = key_c == thr
        pref = jnp.dot(tri, jnp.where(tie, 1.0, 0.0).astype(BF16), preferred_element_type=F32) + offset
        offset = pref[127:128, :]
        chosen = jnp.where(key_c > thr, 1.0, jnp.where(tie, jnp.where(pref <= need, 1.0, 0.0), 0.0))
        sel_ref[c * 128:(c + 1) * 128, :] = jnp.where(causal[c * 128:(c + 1) * 128, :], chosen, 0.0)

    n_kv = k_ref.shape[1]
    grp = q_ref.shape[1] // n_kv
    for j in range(n_kv):
        kj = k_ref[0, j].astype(BF16)
        vTj = vT_ref[0, j].astype(BF16)
        for g in range(grp):
            h = j * grp + g
            sT = lax.dot_general(kj, q_ref[0, h].astype(BF16), _NT, preferred_element_type=F32) * scale
            sT = jnp.where(sel_ref[...] > 0.0, sT, MASK_VALUE)
            m = jnp.max(sT, axis=0, keepdims=True)
            p = jnp.exp(sT - m)
            l = jnp.sum(p, axis=0, keepdims=True)
            oT = jnp.dot(vTj, p.astype(BF16), preferred_element_type=F32)
            o_ref[0, h] = oT / l


def dsa_prompt(q, k, v, qi, ki, w, pos, tq=128):
    b, l, nh, dh = q.shape
    n_idx, n_kv = qi.shape[2], k.shape[2]
    topk = min(TOPK_MAX, l // 4)
    out = pl.pallas_call(
        functools.partial(_dsa_prompt_kernel, topk=topk, scale=dh ** -0.5),
        grid=(b, l // tq),
        in_specs=[
            pl.BlockSpec((1, n_idx, tq, qi.shape[3]), lambda bi, qb: (bi, 0, qb, 0)),
            pl.BlockSpec((1, n_idx, tq), lambda bi, qb: (bi, 0, qb)),
            pl.BlockSpec((1, l, ki.shape[2]), lambda bi, qb: (bi, 0, 0)),
            pl.BlockSpec((1, nh, tq, dh), lambda bi, qb: (bi, 0, qb, 0)),
            pl.BlockSpec((1, n_kv, l, dh), lambda bi, qb: (bi, 0, 0, 0)),
            pl.BlockSpec((1, n_kv, dh, l), lambda bi, qb: (bi, 0, 0, 0)),
        ],
        out_specs=pl.BlockSpec((1, nh, dh, tq), lambda bi, qb: (bi, 0, 0, qb)),
        out_shape=jax.ShapeDtypeStruct((b, nh, dh, l), F32),
        scratch_shapes=[pltpu.VMEM((l, tq), jnp.int32), pltpu.VMEM((l, tq), F32)],
        compiler_params=pltpu.CompilerParams(dimension_semantics=("parallel", "arbitrary"),
                                             vmem_limit_bytes=48 * 1024 * 1024),
        name="dsa_prompt",
    )(qi.transpose(0, 2, 1, 3), w.transpose(0, 2, 1), ki, q.transpose(0, 2, 1, 3),
      k.transpose(0, 2, 1, 3), v.transpose(0, 2, 3, 1))
    return out.transpose(0, 3, 1, 2).reshape(b, l, nh * dh)


VMEM_LIMIT = 48 * 1024 * 1024
PAGES_PER_STEP = 16


def _page_specs(layer, n_pages, pg, width):
    def spec(j):
        return pl.BlockSpec((1, 1, PAGE_SIZE, width),
                            lambda b, g, pt: (layer, pt[b * n_pages + g * pg + j], 0, 0))
    return [spec(j) for j in range(pg)]


def _sidx_kernel(pt_ref, qi_ref, w_ref, kn_ref, *rest, pg):
    page_refs, (sc_ref, scn_ref) = rest[:pg], rest[pg:]
    n_tok = sc_ref.shape[1]
    qi = qi_ref[0].astype(BF16)
    w = w_ref[0]

    def score(keys):
        d = lax.dot_general(qi, keys.astype(BF16), _NT, preferred_element_type=F32)
        r = w * jnp.maximum(d, 0.0)
        acc = jnp.zeros((n_tok, PAGE_SIZE), F32)
        for h in range(r.shape[0] // n_tok):
            acc = acc + r[h * n_tok:(h + 1) * n_tok]
        return acc

    for j in range(pg):
        sc_ref[0, :, j * PAGE_SIZE:(j + 1) * PAGE_SIZE] = score(page_refs[j][0, 0])
    row = lax.broadcasted_iota(jnp.int32, (n_tok, PAGE_SIZE), 0)
    col = lax.broadcasted_iota(jnp.int32, (n_tok, PAGE_SIZE), 1)
    scn_ref[0] = jnp.where(col <= row, score(kn_ref[0]), MASK_VALUE)


def sample_indexer_scores(qi, w, ki_new, pool_idx, layer, page_table, pg):
    db, t, nh, d = qi.shape
    n_pages = page_table.shape[1]
    qi_s = qi.transpose(0, 2, 1, 3).reshape(db, nh * t, d)
    w_s = w.transpose(0, 2, 1).reshape(db, nh * t, 1)
    kn = jnp.pad(ki_new, ((0, 0), (0, PAGE_SIZE - t), (0, 0)))
    grid_spec = pltpu.PrefetchScalarGridSpec(
        num_scalar_prefetch=1,
        grid=(db, n_pages // pg),
        in_specs=[pl.BlockSpec((1, nh * t, d), lambda b, g, pt: (b, 0, 0)),
                  pl.BlockSpec((1, nh * t, 1), lambda b, g, pt: (b, 0, 0)),
                  pl.BlockSpec((1, PAGE_SIZE, d), lambda b, g, pt: (b, 0, 0))]
                 + _page_specs(layer, n_pages, pg, d),
        out_specs=[pl.BlockSpec((1, t, pg * PAGE_SIZE), lambda b, g, pt: (b, 0, g)),
                   pl.BlockSpec((1, t, PAGE_SIZE), lambda b, g, pt: (b, 0, 0))],
    )
    return pl.pallas_call(
        functools.partial(_sidx_kernel, pg=pg),
        grid_spec=grid_spec,
        out_shape=[jax.ShapeDtypeStruct((db, t, n_pages * PAGE_SIZE), F32),
                   jax.ShapeDtypeStruct((db, t, PAGE_SIZE), F32)],
        compiler_params=pltpu.CompilerParams(dimension_semantics=("parallel", "arbitrary"),
                                             vmem_limit_bytes=VMEM_LIMIT),
        name="sample_indexer",
    )(page_table.reshape(-1), qi_s, w_s, kn, *([pool_idx] * pg))


def _ssel_kernel(sc_ref, scn_ref, sel_ref, seln_ref, key_ref, *, topk):
    g, t, p = sc_ref.shape
    rows = g * t
    n_chunks = p // PAGE_SIZE + 1

    def to_key(s):
        s = jnp.where(s == 0.0, 0.0, s)
        bits = pltpu.bitcast(s, jnp.int32)
        return bits ^ ((bits >> 31) & 0x7FFFFFFF)

    key_ref[:, :p] = to_key(sc_ref[...].reshape(rows, p))
    key_ref[:, p:] = to_key(scn_ref[...].reshape(rows, PAGE_SIZE))

    def count_ge(cand):
        return jnp.sum(jnp.where(key_ref[...] >= cand, 1.0, 0.0), axis=1, keepdims=True)

    kf = float(topk)
    t0 = jnp.where(count_ge(jnp.zeros((rows, 1), jnp.int32)) >= kf, 0, INT_MIN).astype(jnp.int32)

    def step(i, thr):
        cand = thr | jnp.left_shift(jnp.int32(1), 30 - i)
        return jnp.where(count_ge(cand) >= kf, cand, thr)

    thr = lax.fori_loop(0, 31, step, t0)
    need = kf - jnp.sum(jnp.where(key_ref[...] > thr, 1.0, 0.0), axis=1, keepdims=True)
    tri = (lax.broadcasted_iota(jnp.int32, (PAGE_SIZE, PAGE_SIZE), 0)
           <= lax.broadcasted_iota(jnp.int32, (PAGE_SIZE, PAGE_SIZE), 1))
    tri = jnp.where(tri, 1.0, 0.0).astype(BF16)
    offset = jnp.zeros((rows, 1), F32)
    trow = lax.broadcasted_iota(jnp.int32, (rows, PAGE_SIZE), 0) & (t - 1)
    tcol = lax.broadcasted_iota(jnp.int32, (rows, PAGE_SIZE), 1)
    for c in range(n_chunks):
        key_c = key_ref[:, c * PAGE_SIZE:(c + 1) * PAGE_SIZE]
        tie = key_c == thr
        pref = jnp.dot(jnp.where(tie, 1.0, 0.0).astype(BF16), tri, preferred_element_type=F32) + offset
        offset = pref[:, PAGE_SIZE - 1:PAGE_SIZE]
        chosen = jnp.where(key_c > thr, 1.0, jnp.where(tie, jnp.where(pref <= need, 1.0, 0.0), 0.0))
        if c < n_chunks - 1:
            sel_ref[:, :, c * PAGE_SIZE:(c + 1) * PAGE_SIZE] = chosen.reshape(g, t, PAGE_SIZE)
        else:
            seln_ref[...] = jnp.where(tcol <= trow, chosen, 0.0).reshape(g, t, PAGE_SIZE)


def sample_select(sc, scn, topk, g=8):
    db, t, p = sc.shape
    assert t & (t - 1) == 0 and db % g == 0
    return pl.pallas_call(
        functools.partial(_ssel_kernel, topk=topk),
        grid=(db // g,),
        in_specs=[pl.BlockSpec((g, t, p), lambda i: (i, 0, 0)),
                  pl.BlockSpec((g, t, PAGE_SIZE), lambda i: (i, 0, 0))],
        out_specs=[pl.BlockSpec((g, t, p), lambda i: (i, 0, 0)),
                   pl.BlockSpec((g, t, PAGE_SIZE), lambda i: (i, 0, 0))],
        out_shape=[jax.ShapeDtypeStruct((db, t, p), F32), jax.ShapeDtypeStruct((db, t, PAGE_SIZE), F32)],
        scratch_shapes=[pltpu.VMEM((g * t, p + PAGE_SIZE), jnp.int32)],
        compiler_params=pltpu.CompilerParams(dimension_semantics=("parallel",), vmem_limit_bytes=VMEM_LIMIT),
        name="sample_select",
    )(sc, scn)


def _online_softmax_step(s, valid, v_list, m_ref, l_ref, acc_ref):
    m_old = m_ref[...]
    m_new = jnp.maximum(m_old, jnp.max(s, axis=1, keepdims=True))
    alpha = jnp.exp(m_old - m_new)
    p = jnp.exp(s - m_new)
    if valid is not None:
        p = jnp.where(valid, p, 0.0)
    l_ref[...] = alpha * l_ref[...] + jnp.sum(p, axis=1, keepdims=True)
    acc = alpha * acc_ref[...]
    pb = p.astype(BF16)
    for j, vj in enumerate(v_list):
        acc = acc + jnp.dot(pb[:, j * PAGE_SIZE:(j + 1) * PAGE_SIZE], vj, preferred_element_type=F32)
    acc_ref[...] = acc
    m_ref[...] = m_new


def _sattn_kernel(pt_ref, q_ref, sel_ref, seln_ref, kn_ref, vn_ref, *rest, pg, scale):
    k_refs, v_refs = rest[:pg], rest[pg:2 * pg]
    o_ref, m_ref, l_ref, acc_ref = rest[2 * pg:]
    gi = pl.program_id(1)
    rep = q_ref.shape[1] // sel_ref.shape[1]
    q = q_ref[0].astype(BF16)

    @pl.when(gi == 0)
    def _():
        m_ref[...] = jnp.full(m_ref.shape, MASK_VALUE, F32)
        l_ref[...] = jnp.zeros(l_ref.shape, F32)
        acc_ref[...] = jnp.zeros(acc_ref.shape, F32)

    def attend(k_list, v_list, sel):
        s = jnp.concatenate([lax.dot_general(q, kk, _NT, preferred_element_type=F32) for kk in k_list], axis=1)
        valid = jnp.concatenate([sel] * rep, axis=0) > 0.0
        s = jnp.where(valid, s * scale, MASK_VALUE)
        _online_softmax_step(s, valid, v_list, m_ref, l_ref, acc_ref)

    attend([r[0, 0].astype(BF16) for r in k_refs], [r[0, 0].astype(BF16) for r in v_refs], sel_ref[0])

    @pl.when(gi == pl.num_programs(1) - 1)
    def _():
        attend([kn_ref[0].astype(BF16)], [vn_ref[0].astype(BF16)], seln_ref[0])
        o_ref[0] = acc_ref[...] / l_ref[...]


def sample_sparse_attention(q, k_new, v_new, sel, seln, pool_k, pool_v, layer, page_table, pg):
    db, t, nh, dh = q.shape
    n_kv = k_new.shape[2]
    grp = nh // n_kv
    n_pages = page_table.shape[1]
    rows = nh * t
    qh = q.transpose(0, 2, 1, 3).reshape(db, n_kv, grp * t, dh)
    q_pad = jnp.concatenate(
        [jnp.pad(qh[:, j], ((0, 0), (0, 0), (j * dh, (n_kv - 1 - j) * dh))) for j in range(n_kv)], axis=1)
    kn = jnp.pad(k_new.reshape(db, t, n_kv * dh), ((0, 0), (0, PAGE_SIZE - t), (0, 0)))
    vn = jnp.pad(v_new.reshape(db, t, n_kv * dh), ((0, 0), (0, PAGE_SIZE - t), (0, 0)))
    width = n_kv * dh
    grid_spec = pltpu.PrefetchScalarGridSpec(
        num_scalar_prefetch=1,
        grid=(db, n_pages // pg),
        in_specs=[pl.BlockSpec((1, rows, width), lambda b, g, pt: (b, 0, 0)),
                  pl.BlockSpec((1, t, pg * PAGE_SIZE), lambda b, g, pt: (b, 0, g)),
                  pl.BlockSpec((1, t, PAGE_SIZE), lambda b, g, pt: (b, 0, 0)),
                  pl.BlockSpec((1, PAGE_SIZE, width), lambda b, g, pt: (b, 0, 0)),
                  pl.BlockSpec((1, PAGE_SIZE, width), lambda b, g, pt: (b, 0, 0))]
                 + _page_specs(layer, n_pages, pg, width) + _page_specs(layer, n_pages, pg, width),
        out_specs=pl.BlockSpec((1, rows, width), lambda b, g, pt: (b, 0, 0)),
        scratch_shapes=[pltpu.VMEM((rows, 1), F32), pltpu.VMEM((rows, 1), F32), pltpu.VMEM((rows, width), F32)],
    )
    out = pl.pallas_call(
        functools.partial(_sattn_kernel, pg=pg, scale=dh ** -0.5),
        grid_spec=grid_spec,
        out_shape=jax.ShapeDtypeStruct((db, rows, width), F32),
        compiler_params=pltpu.CompilerParams(dimension_semantics=("parallel", "arbitrary"),
                                             vmem_limit_bytes=VMEM_LIMIT),
        name="sample_sparse_attention",
    )(page_table.reshape(-1), q_pad, sel, seln, kn, vn, *([pool_k] * pg), *([pool_v] * pg))
    out = out.reshape(db, n_kv, grp, t, n_kv, dh)
    out = jnp.stack([out[:, j, :, :, j] for j in range(n_kv)], axis=1)
    return out.transpose(0, 3, 1, 2, 4).reshape(db, t, nh * dh)


def dsa_sample(q, k_new, v_new, qi, ki_new, w, past, pool_k, pool_v, pool_idx, layer, page_table):
    t = q.shape[1]
    topk = min(TOPK_MAX, (past + t) // 4)
    sc, scn = sample_indexer_scores(qi, w, ki_new, pool_idx, layer, page_table, PAGES_PER_STEP)
    sel, seln = sample_select(sc, scn, topk)
    return sample_sparse_attention(q, k_new, v_new, sel, seln, pool_k, pool_v, layer, page_table, PAGES_PER_STEP)


def _smla_kernel(pt_ref, qa_ref, qr_ref, cn_ref, rn_ref, wuv_ref, *rest, pg, scale):
    c_refs, r_refs = rest[:pg], rest[pg:2 * pg]
    o_ref, m_ref, l_ref, acc_ref = rest[2 * pg:]
    gi = pl.program_id(1)
    n_heads = wuv_ref.shape[0]
    t = qa_ref.shape[1] // n_heads
    qa = qa_ref[0].astype(BF16)
    qr = qr_ref[0].astype(BF16)

    @pl.when(gi == 0)
    def _():
        m_ref[...] = jnp.full(m_ref.shape, MASK_VALUE, F32)
        l_ref[...] = jnp.zeros(l_ref.shape, F32)
        acc_ref[...] = jnp.zeros(acc_ref.shape, F32)

    def attend(c_list, r_list, valid):
        s = jnp.concatenate(
            [lax.dot_general(qa, c, _NT, preferred_element_type=F32)
             + lax.dot_general(qr, r, _NT, preferred_element_type=F32) for c, r in zip(c_list, r_list)], axis=1)
        s = s * scale
        if valid is not None:
            s = jnp.where(valid, s, MASK_VALUE)
        _online_softmax_step(s, valid, c_list, m_ref, l_ref, acc_ref)

    attend([r[0, 0].astype(BF16) for r in c_refs], [r[0, 0].astype(BF16) for r in r_refs], None)

    @pl.when(gi == pl.num_programs(1) - 1)
    def _():
        rows = qa_ref.shape[1]
        tok = lax.broadcasted_iota(jnp.int32, (rows, PAGE_SIZE), 0) & (t - 1)
        col = lax.broadcasted_iota(jnp.int32, (rows, PAGE_SIZE), 1)
        attend([cn_ref[0].astype(BF16)], [rn_ref[0].astype(BF16)], col <= tok)
        o_lat = (acc_ref[...] / l_ref[...]).astype(BF16)
        for h in range(n_heads):
            o_ref[0, h] = jnp.dot(o_lat[h * t:(h + 1) * t], wuv_ref[h].astype(BF16), preferred_element_type=F32)


def sample_mla_attention(q_abs, q_rope, c_new, r_new, w_uv, pool_ckv, pool_ckr, layer, page_table, pg, scale):
    db, t, nh, dl = q_abs.shape
    assert t & (t - 1) == 0
    dr = q_rope.shape[3]
    dv = w_uv.shape[2]
    n_pages = page_table.shape[1]
    rows = nh * t
    qa = q_abs.transpose(0, 2, 1, 3).reshape(db, rows, dl)
    qr = q_rope.transpose(0, 2, 1, 3).reshape(db, rows, dr)
    cn = jnp.pad(c_new, ((0, 0), (0, PAGE_SIZE - t), (0, 0)))
    rn = jnp.pad(r_new, ((0, 0), (0, PAGE_SIZE - t), (0, 0)))
    grid_spec = pltpu.PrefetchScalarGridSpec(
        num_scalar_prefetch=1,
        grid=(db, n_pages // pg),
        in_specs=[pl.BlockSpec((1, rows, dl), lambda b, g, pt: (b, 0, 0)),
                  pl.BlockSpec((1, rows, dr), lambda b, g, pt: (b, 0, 0)),
                  pl.BlockSpec((1, PAGE_SIZE, dl), lambda b, g, pt: (b, 0, 0)),
                  pl.BlockSpec((1, PAGE_SIZE, dr), lambda b, g, pt: (b, 0, 0)),
                  pl.BlockSpec((nh, dl, dv), lambda b, g, pt: (0, 0, 0))]
                 + _page_specs(layer, n_pages, pg, dl) + _page_specs(layer, n_pages, pg, dr),
        out_specs=pl.BlockSpec((1, nh, t, dv), lambda b, g, pt: (b, 0, 0, 0)),
        scratch_shapes=[pltpu.VMEM((rows, 1), F32), pltpu.VMEM((rows, 1), F32), pltpu.VMEM((rows, dl), F32)],
    )
    out = pl.pallas_call(
        functools.partial(_smla_kernel, pg=pg, scale=scale),
        grid_spec=grid_spec,
        out_shape=jax.ShapeDtypeStruct((db, nh, t, dv), F32),
        compiler_params=pltpu.CompilerParams(dimension_semantics=("parallel", "arbitrary"),
                                             vmem_limit_bytes=VMEM_LIMIT),
        name="sample_mla_attention",
    )(page_table.reshape(-1), qa, qr, cn, rn, w_uv.transpose(1, 0, 2), *([pool_ckv] * pg), *([pool_ckr] * pg))
    return out.transpose(0, 2, 1, 3).reshape(db, t, nh * dv)


def mla_attend(q_abs, q_rope, c_kv, k_rope, q_pos, k_pos, w_uv):
    b, t = q_abs.shape[:2]
    s = jnp.einsum('bthr,bsr->bhts', q_abs, c_kv) + jnp.einsum('bthp,bsp->bhts', q_rope, k_rope)
    s = s * ((C_NOPE + C_ROPE) ** -0.5)
    s = jnp.where(k_pos[None, None, None, :] <= q_pos[None, None, :, None], s, MASK_VALUE)
    p = jax.nn.softmax(s, axis=-1)
    o_lat = jnp.einsum('bhts,bsr->bthr', p, c_kv)
    return jnp.einsum('bthr,rhv->bthv', o_lat, w_uv).reshape(b, t, C_HEADS * C_V)


def mla_prompt(q_abs, q_rope, c_kv, k_rope, pos, w_uv):
    l = q_abs.shape[1]
    qb = min(Q_BLOCK, l)

    def body(args):
        qa, qr, qp = args
        return mla_attend(qa, qr, c_kv, k_rope, qp, pos, w_uv)

    return from_blocks(lax.map(body, (to_blocks(q_abs, qb), to_blocks(q_rope, qb), pos.reshape(l // qb, qb))))


def prompt_mixers(h, pos, w_in, lb, a_norm_g, cq_g, w_uq, ckv_g, w_uk, w_uv):
    (a_q, a_logf, a_k, a_v, a_g), (b_q, b_k, b_v, b_qi, b_ki, b_w), (c_qn, c_qr, c_kv, c_kr) = project(
        h, w_in, pos, lb, cq_g, w_uq, ckv_g)
    s0 = jnp.zeros((h.shape[0], A_HEADS, A_DK, A_DV), F32)
    o_a, s_a = hgrn2_recurrence(a_q, a_logf, a_k, a_v, s0)
    o_a = hgrn_output(o_a, a_g, a_norm_g)
    o_b = dsa_prompt(b_q, b_k, b_v, b_qi, b_ki, b_w, pos)
    q_abs = jnp.einsum('bthn,rhn->bthr', c_qn, w_uk)
    o_c = mla_prompt(q_abs, c_qr, c_kv, c_kr, pos, w_uv)
    mix = jnp.concatenate([o_a, o_b, o_c], -1)
    return mix, (b_k, b_v, b_ki, c_kv, c_kr, s_a)


def sample_mixers(h, pos, past, layer, pool_bk, pool_bv, pool_bidx, pool_ckv, pool_ckr, s_prev, page_table,
                  w_in, lb, a_norm_g, cq_g, w_uq, ckv_g, w_uk, w_uv):
    (a_q, a_logf, a_k, a_v, a_g), (b_q, b_k, b_v, b_qi, b_ki, b_w), (c_qn, c_qr, c_kv, c_kr) = project(
        h, w_in, pos, lb, cq_g, w_uq, ckv_g)
    o_a, s_a = hgrn2_recurrence(a_q, a_logf, a_k, a_v, s_prev)
    o_a = hgrn_output(o_a, a_g, a_norm_g)
    o_b = dsa_sample(b_q, b_k, b_v, b_qi, b_ki, b_w, past, pool_bk, pool_bv, pool_bidx, layer, page_table)
    q_abs = jnp.einsum('bthn,rhn->bthr', c_qn, w_uk)
    o_c = sample_mla_attention(q_abs, c_qr, c_kv, c_kr, w_uv, pool_ckv, pool_ckr, layer, page_table,
                               PAGES_PER_STEP, (C_NOPE + C_ROPE) ** -0.5)
    mix = jnp.concatenate([o_a, o_b, o_c], -1)
    return mix, (b_k, b_v, b_ki, c_kv, c_kr, s_a)


def finish_layer(x, mix, w_o, ln1_g, ln1_b, w_gate, w_up, w_down, ln2_g, ln2_b):
    x = layer_norm(DEEPNORM_ALPHA * x + _mm3(mix, w_o), ln1_g, ln1_b)
    ffn = _mm3(jax.nn.silu(_mm3(x, w_gate)) * _mm3(x, w_up), w_down)
    return layer_norm(DEEPNORM_ALPHA * x + ffn, ln2_g, ln2_b)


def kernel(x_prompt, x_sample, cache_b_k, cache_b_v, cache_b_idx, cache_c_kv, cache_c_kr, state_a,
           page_table, ln_in_g, ln_in_b, w_in, hgrn_lb_logits, hgrn_norm_g, mla_q_norm_g, mla_w_uq,
           mla_kv_norm_g, mla_w_uk, mla_w_uv, w_o, ln1_g, ln1_b, w_gate, w_up, w_down, ln2_g, ln2_b):
    lower_bounds = hgrn_lower_bounds(hgrn_lb_logits)
    past = page_table.shape[1] * PAGE_SIZE
    pos_p = jnp.arange(x_prompt.shape[1], dtype=jnp.int32)
    pos_s = past + jnp.arange(x_sample.shape[1], dtype=jnp.int32)
    xp = layer_norm(x_prompt, ln_in_g, ln_in_b)
    xs = layer_norm(x_sample, ln_in_g, ln_in_b)
    kv_width = B_KV_HEADS * B_HEAD_DIM
    pool_bk = cache_b_k.reshape(cache_b_k.shape[:3] + (kv_width,))
    pool_bv = cache_b_v.reshape(cache_b_v.shape[:3] + (kv_width,))
    new_p =([], [], [], [], [], [])
    new_s = ([], [], [], [], [], [])
    for l in range(DEPTH):
        mix_w = (w_in[l], lower_bounds[l], hgrn_norm_g[l], mla_q_norm_g[l], mla_w_uq[l],
                 mla_kv_norm_g[l], mla_w_uk[l], mla_w_uv[l])
        ffn_w = (w_o[l], ln1_g[l], ln1_b[l], w_gate[l], w_up[l], w_down[l], ln2_g[l], ln2_b[l])
        mix_p, st_p = prompt_mixers(xp, pos_p, *mix_w)
        mix_s, st_s = sample_mixers(xs, pos_s, past, l, pool_bk, pool_bv, cache_b_idx,
                                    cache_c_kv, cache_c_kr, state_a[l], page_table, *mix_w)
        xp = finish_layer(xp, mix_p, *ffn_w)
        xs = finish_layer(xs, mix_s, *ffn_w)
        for lst, st in zip(new_p, st_p):
            lst.append(st)
        for lst, st in zip(new_s, st_s):
            lst.append(st)
    b_k_p, b_v_p, b_idx_p, c_kv_p, c_kr_p, s_a_p = [jnp.stack(v, 0) for v in new_p]
    b_k_s, b_v_s, b_idx_s, c_kv_s, c_kr_s, s_a_s = [jnp.stack(v, 0) for v in new_s]
    return (xp, xs, b_k_p, b_v_p, b_idx_p, c_kv_p, c_kr_p, s_a_p,
            b_k_s, b_v_s, b_idx_s, c_kv_s, c_kr_s, s_a_s)
```

```python
import functools
import math

import jax
import jax.numpy as jnp
import numpy as np
from jax import lax
from jax.experimental import pallas as pl
from jax.experimental.pallas import tpu as pltpu

D_MODEL = 1024
DEPTH = 4
PAGE_SIZE = 128
A_HEADS = 4
A_DK = 128
A_DV = 128
A_CHUNK = 32
B_HEADS = 4
B_KV_HEADS = 2
B_HEAD_DIM = 64
IDX_HEADS = 8
IDX_DIM = 64
TOPK_MAX = 256
C_HEADS = 4
C_NOPE = 64
C_ROPE = 32
C_V = 64
C_Q_LORA = 256
C_KV_LORA = 128
ROPE_THETA = 10000.0
Q_BLOCK = 128
EPS = 1e-5
MASK_VALUE = -1e30
DEEPNORM_ALPHA = (2 * DEPTH) ** 0.25
IN_SPLITS = (A_HEADS * A_DK, A_HEADS * A_DK, A_HEADS * A_DV, A_HEADS * A_DV,
             B_HEADS * B_HEAD_DIM, B_KV_HEADS * B_HEAD_DIM, B_KV_HEADS * B_HEAD_DIM,
             IDX_HEADS * IDX_DIM, IDX_DIM, IDX_HEADS,
             C_Q_LORA, C_KV_LORA, C_ROPE)
F32 = jnp.float32
BF16 = jnp.bfloat16


def _mm_kernel(x_ref, w_ref, o_ref):
    o_ref[...] = jnp.dot(x_ref[...], w_ref[...], preferred_element_type=F32)


def _matmul(x, w, tm=1024, tn=512):
    m, k = x.shape
    n = w.shape[1]
    tm = min(tm, m)
    tn = min(tn, n)
    return pl.pallas_call(
        _mm_kernel,
        grid=(pl.cdiv(m, tm), pl.cdiv(n, tn)),
        in_specs=[pl.BlockSpec((tm, k), lambda i, j: (i, 0)),
                  pl.BlockSpec((k, tn), lambda i, j: (0, j))],
        out_specs=pl.BlockSpec((tm, tn), lambda i, j: (i, j)),
        out_shape=jax.ShapeDtypeStruct((m, n), F32),
        compiler_params=pltpu.CompilerParams(dimension_semantics=("parallel", "parallel"),
                                             vmem_limit_bytes=48 * 1024 * 1024),
        name="dense_matmul",
    )(x, w)


def _mm3(h, w):
    b, l, d = h.shape
    return _matmul(h.reshape(b * l, d), w).reshape(b, l, w.shape[1])


def _ffn_up_kernel(x_ref, wg_ref, wu_ref, h_ref):
    x = x_ref[...]
    g = jnp.dot(x, wg_ref[...], preferred_element_type=F32)
    u = jnp.dot(x, wu_ref[...], preferred_element_type=F32)
    h_ref[...] = (g * jax.nn.sigmoid(g) * u).astype(BF16)


def ffn_up(x, w_gate, w_up, tm=512):
    m, d = x.shape
    f = w_gate.shape[1]
    tm = min(tm, m)
    tn = f // 2
    assert m % tm == 0 and tn % 128 == 0
    return pl.pallas_call(
        _ffn_up_kernel,
        grid=(m // tm, 2),
        in_specs=[pl.BlockSpec((tm, d), lambda i, j: (i, 0)),
                  pl.BlockSpec((d, tn), lambda i, j: (0, j)),
                  pl.BlockSpec((d, tn), lambda i, j: (0, j))],
        out_specs=pl.BlockSpec((tm, tn), lambda i, j: (i, j)),
        out_shape=jax.ShapeDtypeStruct((m, f), BF16),
        compiler_params=pltpu.CompilerParams(dimension_semantics=("parallel", "parallel"),
                                             vmem_limit_bytes=48 * 1024 * 1024),
        name="ffn_up",
    )(x, w_gate, w_up)


def _mm_res_ln_kernel(a_ref, w_ref, res_ref, g_ref, b_ref, o_ref, ob_ref):
    y = DEEPNORM_ALPHA * res_ref[...] + jnp.dot(a_ref[...], w_ref[...], preferred_element_type=F32)
    mu = jnp.mean(y, axis=-1, keepdims=True)
    var = jnp.mean(jnp.square(y - mu), axis=-1, keepdims=True)
    out = (y - mu) * lax.rsqrt(var + EPS) * g_ref[...] + b_ref[...]
    o_ref[...] = out
    ob_ref[...] = out.astype(BF16)


def matmul_res_ln(a, w, res, g, b, tm=512):
    m, k = a.shape
    d = w.shape[1]
    tm = min(tm, m)
    assert m % tm == 0
    return pl.pallas_call(
        _mm_res_ln_kernel,
        grid=(m // tm,),
        in_specs=[pl.BlockSpec((tm, k), lambda i: (i, 0)),
                  pl.BlockSpec((k, d), lambda i: (0, 0)),
                  pl.BlockSpec((tm, d), lambda i: (i, 0)),
                  pl.BlockSpec((1, d), lambda i: (0, 0)),
                  pl.BlockSpec((1, d), lambda i: (0, 0))],
        out_specs=[pl.BlockSpec((tm, d), lambda i: (i, 0)), pl.BlockSpec((tm, d), lambda i: (i, 0))],
        out_shape=[jax.ShapeDtypeStruct((m, d), F32), jax.ShapeDtypeStruct((m, d), BF16)],
        compiler_params=pltpu.CompilerParams(dimension_semantics=("parallel",),
                                             vmem_limit_bytes=48 * 1024 * 1024),
        name="matmul_res_ln",
    )(a, w, res, g.reshape(1, d), b.reshape(1, d))


def layer_norm(x, g, b):
    mu = jnp.mean(x, -1, keepdims=True)
    var = jnp.mean(jnp.square(x - mu), -1, keepdims=True)
    return (x - mu) * lax.rsqrt(var + EPS) * g + b


def rms_norm(x, g):
    return x * lax.rsqrt(jnp.mean(jnp.square(x), -1, keepdims=True) + EPS) * g


def rope(x, pos):
    half = x.shape[-1] // 2
    inv = ROPE_THETA ** (-jnp.arange(half, dtype=F32) / half)
    ang = pos.astype(F32)[:, None] * inv[None, :]
    shape = (ang.shape[0],) + (1,) * (x.ndim - 3) + (half,)
    cos = jnp.cos(ang).reshape(shape)
    sin = jnp.sin(ang).reshape(shape)
    x1, x2 = x[..., :half], x[..., half:]
    return jnp.concatenate([x1 * cos - x2 * sin, x2 * cos + x1 * sin], -1)


def hgrn_lower_bounds(logits):
    p = jax.nn.softmax(logits.astype(F32), axis=0)
    return jnp.clip(jnp.cumsum(p, axis=0) - p[0:1], 0.0, 1.0 - 1e-4)


def to_blocks(t, qb):
    b, l = t.shape[:2]
    return t.reshape((b, l // qb, qb) + t.shape[2:]).swapaxes(0, 1)


def from_blocks(t):
    nb, b, qb = t.shape[:3]
    return t.swapaxes(0, 1).reshape((b, nb * qb) + t.shape[3:])


def project(h, w_in, pos, lb, cq_g, w_uq, ckv_g):
    b, l = h.shape[:2]
    offs = [int(o) for o in np.cumsum(IN_SPLITS)[:-1]]
    aq, af, ai, ag, bq, bk, bv, bqi, bki, bw, cdq, cdkv, ckr = jnp.split(_mm3(h, w_in), offs, axis=-1)
    z = af.reshape(b, l, A_HEADS, A_DK)
    lbh = lb.reshape(A_HEADS, A_DK)
    f = lbh + (1.0 - lbh) * jax.nn.sigmoid(z)
    a_logf = jnp.log(jnp.maximum(f, 1e-30))
    a_k = 1.0 - f
    a_q = aq.reshape(b, l, A_HEADS, A_DK)
    a_v = ai.reshape(b, l, A_HEADS, A_DV)
    b_q = rope(bq.reshape(b, l, B_HEADS, B_HEAD_DIM), pos)
    b_k = rope(bk.reshape(b, l, B_KV_HEADS, B_HEAD_DIM), pos)
    b_v = bv.reshape(b, l, B_KV_HEADS, B_HEAD_DIM)
    b_qi = rope(bqi.reshape(b, l, IDX_HEADS, IDX_DIM), pos)
    b_ki = rope(bki[:, :, None, :], pos)[:, :, 0]
    b_w = bw * (IDX_HEADS ** -0.5 * IDX_DIM ** -0.5)
    cq = jnp.einsum('blr,rhd->blhd', rms_norm(cdq, cq_g), w_uq)
    c_qn = cq[..., :C_NOPE]
    c_qr = rope(cq[..., C_NOPE:], pos)
    c_kv = rms_norm(cdkv, ckv_g)
    c_kr = rope(ckr[:, :, None, :], pos)[:, :, 0]
    return (a_q, a_logf, a_k, a_v, ag), (b_q, b_k, b_v, b_qi, b_ki, b_w), (c_qn, c_qr, c_kv, c_kr)


def hgrn2_recurrence(q, log_f, k, v, s0):
    b, l, h, _ = q.shape
    c = math.gcd(l, A_CHUNK)
    n = l // c

    def blocks(t):
        return t.reshape(b, n, c, h, t.shape[-1]).transpose(1, 0, 3, 2, 4)

    causal = jnp.tril(jnp.ones((c, c), bool))

    def step(s, inp):
        qc, lf, kc, vc = inp
        cum = jnp.cumsum(lf, axis=2)
        diff = cum[:, :, :, None, :] - cum[:, :, None, :, :]
        decay = jnp.exp(jnp.where(causal[:, :, None], diff, MASK_VALUE))
        attn = jnp.einsum('bhtd,bhtsd,bhsd->bhts', qc, decay, kc)
        o = attn @ vc + jnp.einsum('bhtd,bhdv->bhtv', qc * jnp.exp(cum), s)
        last = cum[:, :, -1:, :]
        s_new = jnp.exp(last)[:, :, 0, :, None] * s + jnp.einsum('bhsd,bhsv->bhdv', kc * jnp.exp(last - cum), vc)
        return s_new, o

    s_fin, o = lax.scan(step, s0, (blocks(q), blocks(log_f), blocks(k), blocks(v)))
    return o.transpose(1, 0, 3, 2, 4).reshape(b, l, h, -1), s_fin


def hgrn_output(o, g, norm_g):
    b, l = o.shape[:2]
    return rms_norm(o, norm_g).reshape(b, l, -1) * jax.nn.silu(g)


def indexer_scores(qi, w, ki, q_pos, k_pos):
    score = jnp.zeros(qi.shape[:2] + (ki.shape[1],), F32)
    for hh in range(IDX_HEADS):
        dots = jnp.einsum('btd,bsd->bts', qi[:, :, hh], ki)
        score = score + w[:, :, hh, None] * jax.nn.relu(dots)
    return jnp.where(k_pos[None, None, :] <= q_pos[None, :, None], score, MASK_VALUE)


def sparse_attention(q, kg, vg, valid):
    b, t = q.shape[:2]
    qg = q.reshape(b, t, B_KV_HEADS, B_HEADS // B_KV_HEADS, B_HEAD_DIM)
    s = jnp.einsum('btkgd,btjkd->btkgj', qg, kg) * (B_HEAD_DIM ** -0.5)
    s = jnp.where(valid[:, :, None, None, :], s, MASK_VALUE)
    p = jax.nn.softmax(s, axis=-1)
    return jnp.einsum('btkgj,btjkd->btkgd', p, vg).reshape(b, t, B_HEADS * B_HEAD_DIM)


def gather_rows(rows, idx):
    return jax.vmap(lambda r, i: r[i])(rows, idx)


INT_MIN = -2 ** 31
_NT = (((1,), (1,)), ((), ()))


def _dsa_prompt_kernel(qi_ref, wT_ref, ki_ref, q_ref, k_ref, vT_ref, o_ref, key_ref, sel_ref, *, topk, scale):
    n_idx, tq = qi_ref.shape[1], qi_ref.shape[2]
    L = ki_ref.shape[1]
    qb = pl.program_id(1)
    ki = ki_ref[0].astype(BF16)
    acc = jnp.zeros((L, tq), F32)
    for h in range(n_idx):
        d = lax.dot_general(ki, qi_ref[0, h].astype(BF16), _NT, preferred_element_type=F32)
        acc = acc + wT_ref[0, h:h + 1, :] * jnp.maximum(d, 0.0)
    kpos = lax.broadcasted_iota(jnp.int32, (L, tq), 0)
    qpos = qb * tq + lax.broadcasted_iota(jnp.int32, (L, tq), 1)
    causal = kpos <= qpos
    s = jnp.where(causal, acc, MASK_VALUE)
    s = jnp.where(s == 0.0, 0.0, s)
    bits = pltpu.bitcast(s, jnp.int32)
    key_ref[...] = bits ^ ((bits >> 31) & 0x7FFFFFFF)

    def count_ge(cand):
        return jnp.sum(jnp.where(key_ref[...] >= cand, 1.0, 0.0), axis=0, keepdims=True)

    kf = float(topk)
    t0 = jnp.where(count_ge(jnp.zeros((1, tq), jnp.int32)) >= kf, 0, INT_MIN).astype(jnp.int32)

    def step(i, t):
        cand = t | jnp.left_shift(jnp.int32(1), 30 - i)
        return jnp.where(count_ge(cand) >= kf, cand, t)

    thr = lax.fori_loop(0, 31, step, t0)
    need = kf - jnp.sum(jnp.where(key_ref[...] > thr, 1.0, 0.0), axis=0, keepdims=True)
    tri = (lax.broadcasted_iota(jnp.int32, (128, 128), 0) >= lax.broadcasted_iota(jnp.int32, (128, 128), 1))
    tri = jnp.where(tri, 1.0, 0.0).astype(BF16)
    offset = jnp.zeros((1, tq), F32)
    for c in range(L // 128):
        key_c = key_ref[c * 128:(c + 1) * 128, :]
        tie = key_c == thr
        pref = jnp.dot(tri, jnp.where(tie, 1.0, 0.0).astype(BF16), preferred_element_type=F32) + offset
        offset = pref[127:128, :]
        chosen = jnp.where(key_c > thr, 1.0, jnp.where(tie, jnp.where(pref <= need, 1.0, 0.0), 0.0))
        sel_ref[c * 128:(c + 1) * 128, :] = jnp.where(causal[c * 128:(c + 1) * 128, :], chosen, 0.0)

    n_kv = k_ref.shape[1]
    grp = q_ref.shape[1] // n_kv
    for j in range(n_kv):
        kj = k_ref[0, j].astype(BF16)
        vTj = vT_ref[0, j].astype(BF16)
        for g in range(grp):
            h = j * grp + g
            sT = lax.dot_general(kj, q_ref[0, h].astype(BF16), _NT, preferred_element_type=F32) * scale
            sT = jnp.where(sel_ref[...] > 0.0, sT, MASK_VALUE)
            m = jnp.max(sT, axis=0, keepdims=True)
            p = jnp.exp(sT - m)
            l = jnp.sum(p, axis=0, keepdims=True)
            oT = jnp.dot(vTj, p.astype(BF16), preferred_element_type=F32)
            o_ref[0, h] = oT / l


def dsa_prompt(q, k, v, qi, ki, w, pos, tq=128):
    b, l, nh, dh = q.shape
    n_idx, n_kv = qi.shape[2], k.shape[2]
    topk = min(TOPK_MAX, l // 4)
    out = pl.pallas_call(
        functools.partial(_dsa_prompt_kernel, topk=topk, scale=dh ** -0.5),
        grid=(b, l // tq),
        in_specs=[
            pl.BlockSpec((1, n_idx, tq, qi.shape[3]), lambda bi, qb: (bi, 0, qb, 0)),
            pl.BlockSpec((1, n_idx, tq), lambda bi, qb: (bi, 0, qb)),
            pl.BlockSpec((1, l, ki.shape[2]), lambda bi, qb: (bi, 0, 0)),
            pl.BlockSpec((1, nh, tq, dh), lambda bi, qb: (bi, 0, qb, 0)),
            pl.BlockSpec((1, n_kv, l, dh), lambda bi, qb: (bi, 0, 0, 0)),
            pl.BlockSpec((1, n_kv, dh, l), lambda bi, qb: (bi, 0, 0, 0)),
        ],
        out_specs=pl.BlockSpec((1, nh, dh, tq), lambda bi, qb: (bi, 0, 0, qb)),
        out_shape=jax.ShapeDtypeStruct((b, nh, dh, l), F32),
        scratch_shapes=[pltpu.VMEM((l, tq), jnp.int32), pltpu.VMEM((l, tq), F32)],
        compiler_params=pltpu.CompilerParams(dimension_semantics=("parallel", "arbitrary"),
                                             vmem_limit_bytes=48 * 1024 * 1024),
        name="dsa_prompt",
    )(qi.transpose(0, 2, 1, 3), w.transpose(0, 2, 1), ki, q.transpose(0, 2, 1, 3),
      k.transpose(0, 2, 1, 3), v.transpose(0, 2, 3, 1))
    return out.transpose(0, 3, 1, 2).reshape(b, l, nh * dh)


VMEM_LIMIT = 48 * 1024 * 1024
PAGES_PER_STEP = 16


def _page_specs(layer, n_pages, pg, rows, cols):
    def spec(j):
        return pl.BlockSpec((1, 1, rows, cols),
                            lambda b, g, pt: (layer, pt[b * n_pages + g * pg + j], 0, 0))
    return [spec(j) for j in range(pg)]


def _tokens_last(x_new):
    t = x_new.shape[1]
    return jnp.pad(x_new.transpose(0, 2, 1), ((0, 0), (0, 0), (0, PAGE_SIZE - t)))


def _sidx_kernel(pt_ref, qi_ref, w_ref, kn_ref, *rest, pg):
    page_refs, (sc_ref, scn_ref) = rest[:pg], rest[pg:]
    n_tok = sc_ref.shape[1]
    qi = qi_ref[0].astype(BF16)
    w = w_ref[0]

    def score(keys_t):
        d = jnp.dot(qi, keys_t.astype(BF16), preferred_element_type=F32)
        r = w * jnp.maximum(d, 0.0)
        acc = jnp.zeros((n_tok, PAGE_SIZE), F32)
        for h in range(r.shape[0] // n_tok):
            acc = acc + r[h * n_tok:(h + 1) * n_tok]
        return acc

    for j in range(pg):
        sc_ref[0, :, j * PAGE_SIZE:(j + 1) * PAGE_SIZE] = score(page_refs[j][0, 0])
    row = lax.broadcasted_iota(jnp.int32, (n_tok, PAGE_SIZE), 0)
    col = lax.broadcasted_iota(jnp.int32, (n_tok, PAGE_SIZE), 1)
    scn_ref[0] = jnp.where(col <= row, score(kn_ref[0]), MASK_VALUE)


def sample_indexer_scores(qi, w, ki_new, pool_idx, layer, page_table, pg):
    db, t, nh, d = qi.shape
    n_pages = page_table.shape[1]
    qi_s = qi.transpose(0, 2, 1, 3).reshape(db, nh * t, d)
    w_s = w.transpose(0, 2, 1).reshape(db, nh * t, 1)
    kn = _tokens_last(ki_new)
    grid_spec = pltpu.PrefetchScalarGridSpec(
        num_scalar_prefetch=1,
        grid=(db, n_pages // pg),
        in_specs=[pl.BlockSpec((1, nh * t, d), lambda b, g, pt: (b, 0, 0)),
                  pl.BlockSpec((1, nh * t, 1), lambda b, g, pt: (b, 0, 0)),
                  pl.BlockSpec((1, d, PAGE_SIZE), lambda b, g, pt: (b, 0, 0))]
                 + _page_specs(layer, n_pages, pg, d, PAGE_SIZE),
        out_specs=[pl.BlockSpec((1, t, pg * PAGE_SIZE), lambda b, g, pt: (b, 0, g)),
                   pl.BlockSpec((1, t, PAGE_SIZE), lambda b, g, pt: (b, 0, 0))],
    )
    return pl.pallas_call(
        functools.partial(_sidx_kernel, pg=pg),
        grid_spec=grid_spec,
        out_shape=[jax.ShapeDtypeStruct((db, t, n_pages * PAGE_SIZE), F32),
                   jax.ShapeDtypeStruct((db, t, PAGE_SIZE), F32)],
        compiler_params=pltpu.CompilerParams(dimension_semantics=("parallel", "arbitrary"),
                                             vmem_limit_bytes=VMEM_LIMIT),
        name="sample_indexer",
    )(page_table.reshape(-1), qi_s, w_s, kn, *([pool_idx] * pg))


def _ssel_kernel(sc_ref, scn_ref, sel_ref, seln_ref, key_ref, *, topk):
    g, t, p = sc_ref.shape
    rows = g * t
    n_chunks = p // PAGE_SIZE + 1

    def to_key(s):
        s = jnp.where(s == 0.0, 0.0, s)
        bits = pltpu.bitcast(s, jnp.int32)
        return bits ^ ((bits >> 31) & 0x7FFFFFFF)

    key_ref[:, :p] = to_key(sc_ref[...].reshape(rows, p))
    key_ref[:, p:] = to_key(scn_ref[...].reshape(rows, PAGE_SIZE))

    def count_ge(cand):
        return jnp.sum(jnp.where(key_ref[...] >= cand, 1.0, 0.0), axis=1, keepdims=True)

    kf = float(topk)
    t0 = jnp.where(count_ge(jnp.zeros((rows, 1), jnp.int32)) >= kf, 0, INT_MIN).astype(jnp.int32)

    def step(i, thr):
        cand = thr | jnp.left_shift(jnp.int32(1), 30 - i)
        return jnp.where(count_ge(cand) >= kf, cand, thr)

    thr = lax.fori_loop(0, 31, step, t0)
    need = kf - jnp.sum(jnp.where(key_ref[...] > thr, 1.0, 0.0), axis=1, keepdims=True)
    tri = (lax.broadcasted_iota(jnp.int32, (PAGE_SIZE, PAGE_SIZE), 0)
           <= lax.broadcasted_iota(jnp.int32, (PAGE_SIZE, PAGE_SIZE), 1))
    tri = jnp.where(tri, 1.0, 0.0).astype(BF16)
    offset = jnp.zeros((rows, 1), F32)
    trow = lax.broadcasted_iota(jnp.int32, (rows, PAGE_SIZE), 0) & (t - 1)
    tcol = lax.broadcasted_iota(jnp.int32, (rows, PAGE_SIZE), 1)
    for c in range(n_chunks):
        key_c = key_ref[:, c * PAGE_SIZE:(c + 1) * PAGE_SIZE]
        tie = key_c == thr
        pref = jnp.dot(jnp.where(tie, 1.0, 0.0).astype(BF16), tri, preferred_element_type=F32) + offset
        offset = pref[:, PAGE_SIZE - 1:PAGE_SIZE]
        chosen = jnp.where(key_c > thr, 1.0, jnp.where(tie, jnp.where(pref <= need, 1.0, 0.0), 0.0))
        if c < n_chunks - 1:
            sel_ref[:, :, c * PAGE_SIZE:(c + 1) * PAGE_SIZE] = chosen.reshape(g, t, PAGE_SIZE)
        else:
            seln_ref[...] = jnp.where(tcol <= trow, chosen, 0.0).reshape(g, t, PAGE_SIZE)


def sample_select(sc, scn, topk, g=8):
    db, t, p = sc.shape
    assert t & (t - 1) == 0 and db % g == 0
    return pl.pallas_call(
        functools.partial(_ssel_kernel, topk=topk),
        grid=(db // g,),
        in_specs=[pl.BlockSpec((g, t, p), lambda i: (i, 0, 0)),
                  pl.BlockSpec((g, t, PAGE_SIZE), lambda i: (i, 0, 0))],
        out_specs=[pl.BlockSpec((g, t, p), lambda i: (i, 0, 0)),
                   pl.BlockSpec((g, t, PAGE_SIZE), lambda i: (i, 0, 0))],
        out_shape=[jax.ShapeDtypeStruct((db, t, p), F32), jax.ShapeDtypeStruct((db, t, PAGE_SIZE), F32)],
        scratch_shapes=[pltpu.VMEM((g * t, p + PAGE_SIZE), jnp.int32)],
        compiler_params=pltpu.CompilerParams(dimension_semantics=("parallel",), vmem_limit_bytes=VMEM_LIMIT),
        name="sample_select",
    )(sc, scn)


def _online_softmax_step(s, valid, v_list, m_ref, l_ref, acc_ref, values_tokens_last=False):
    m_old = m_ref[...]
    m_new = jnp.maximum(m_old, jnp.max(s, axis=1, keepdims=True))
    alpha = jnp.exp(m_old - m_new)
    p = jnp.exp(s - m_new)
    if valid is not None:
        p = jnp.where(valid, p, 0.0)
    l_ref[...] = alpha * l_ref[...] + jnp.sum(p, axis=1, keepdims=True)
    acc = alpha * acc_ref[...]
    pb = p.astype(BF16)
    for j, vj in enumerate(v_list):
        pj = pb[:, j * PAGE_SIZE:(j + 1) * PAGE_SIZE]
        if values_tokens_last:
            acc = acc + lax.dot_general(pj, vj, _NT, preferred_element_type=F32)
        else:
            acc = acc + jnp.dot(pj, vj, preferred_element_type=F32)
    acc_ref[...] = acc
    m_ref[...] = m_new


def _sattn_kernel(pt_ref, q_ref, sel_ref, seln_ref, kn_ref, vn_ref, *rest, pg, scale):
    k_refs, v_refs = rest[:pg], rest[pg:2 * pg]
    o_ref, m_ref, l_ref, acc_ref = rest[2 * pg:]
    gi = pl.program_id(1)
    rep = q_ref.shape[1] // sel_ref.shape[1]
    q = q_ref[0].astype(BF16)

    @pl.when(gi == 0)
    def _():
        m_ref[...] = jnp.full(m_ref.shape, MASK_VALUE, F32)
        l_ref[...] = jnp.zeros(l_ref.shape, F32)
        acc_ref[...] = jnp.zeros(acc_ref.shape, F32)

    def attend(k_list, v_list, sel):
        s = jnp.concatenate([jnp.dot(q, kk, preferred_element_type=F32) for kk in k_list], axis=1)
        valid = jnp.concatenate([sel] * rep, axis=0) > 0.0
        s = jnp.where(valid, s * scale, MASK_VALUE)
        _online_softmax_step(s, valid, v_list, m_ref, l_ref, acc_ref, values_tokens_last=True)

    attend([r[0, 0].astype(BF16) for r in k_refs], [r[0, 0].astype(BF16) for r in v_refs], sel_ref[0])

    @pl.when(gi == pl.num_programs(1) - 1)
    def _():
        attend([kn_ref[0].astype(BF16)], [vn_ref[0].astype(BF16)], seln_ref[0])
        o_ref[0] = acc_ref[...] / l_ref[...]


def sample_sparse_attention(q, k_new, v_new, sel, seln, pool_k, pool_v, layer, page_table, pg):
    db, t, nh, dh = q.shape
    n_kv = k_new.shape[2]
    grp = nh // n_kv
    n_pages = page_table.shape[1]
    rows = nh * t
    qh = q.transpose(0, 2, 1, 3).reshape(db, n_kv, grp * t, dh)
    q_pad = jnp.concatenate(
        [jnp.pad(qh[:, j], ((0, 0), (0, 0), (j * dh, (n_kv - 1 - j) * dh))) for j in range(n_kv)], axis=1)
    kn = _tokens_last(k_new.reshape(db, t, n_kv * dh))
    vn = _tokens_last(v_new.reshape(db, t, n_kv * dh))
    width = n_kv * dh
    grid_spec = pltpu.PrefetchScalarGridSpec(
        num_scalar_prefetch=1,
        grid=(db, n_pages // pg),
        in_specs=[pl.BlockSpec((1, rows, width), lambda b, g, pt: (b, 0, 0)),
                  pl.BlockSpec((1, t, pg * PAGE_SIZE), lambda b, g, pt: (b, 0, g)),
                  pl.BlockSpec((1, t, PAGE_SIZE), lambda b, g, pt: (b, 0, 0)),
                  pl.BlockSpec((1, width, PAGE_SIZE), lambda b, g, pt: (b, 0, 0)),
                  pl.BlockSpec((1, width, PAGE_SIZE), lambda b, g, pt: (b, 0, 0))]
                 + _page_specs(layer, n_pages, pg, width, PAGE_SIZE)
                 + _page_specs(layer, n_pages, pg, width, PAGE_SIZE),
        out_specs=pl.BlockSpec((1, rows, width), lambda b, g, pt: (b, 0, 0)),
        scratch_shapes=[pltpu.VMEM((rows, 1), F32), pltpu.VMEM((rows, 1), F32), pltpu.VMEM((rows, width), F32)],
    )
    out = pl.pallas_call(
        functools.partial(_sattn_kernel, pg=pg, scale=dh ** -0.5),
        grid_spec=grid_spec,
        out_shape=jax.ShapeDtypeStruct((db, rows, width), F32),
        compiler_params=pltpu.CompilerParams(dimension_semantics=("parallel", "arbitrary"),
                                             vmem_limit_bytes=VMEM_LIMIT),
        name="sample_sparse_attention",
    )(page_table.reshape(-1), q_pad, sel, seln, kn, vn, *([pool_k] * pg), *([pool_v] * pg))
    out = out.reshape(db, n_kv, grp, t, n_kv, dh)
    out = jnp.stack([out[:, j, :, :, j] for j in range(n_kv)], axis=1)
    return out.transpose(0, 3, 1, 2, 4).reshape(db, t, nh * dh)


def dsa_sample(q, k_new, v_new, qi, ki_new, w, past, pool_k, pool_v, pool_idx, layer, page_table):
    t = q.shape[1]
    topk = min(TOPK_MAX, (past + t) // 4)
    sc, scn = sample_indexer_scores(qi, w, ki_new, pool_idx, layer, page_table, PAGES_PER_STEP)
    sel, seln = sample_select(sc, scn, topk)
    return sample_sparse_attention(q, k_new, v_new, sel, seln, pool_k, pool_v, layer, page_table, PAGES_PER_STEP)


def _smla_kernel(pt_ref, qa_ref, qr_ref, cn_ref, rn_ref, wuv_ref, *rest, pg, scale):
    c_refs, r_refs = rest[:pg], rest[pg:2 * pg]
    o_ref, m_ref, l_ref, acc_ref = rest[2 * pg:]
    gi = pl.program_id(1)
    n_heads = wuv_ref.shape[0]
    t = qa_ref.shape[1] // n_heads
    qa = qa_ref[0].astype(BF16)
    qr = qr_ref[0].astype(BF16)

    @pl.when(gi == 0)
    def _():
        m_ref[...] = jnp.full(m_ref.shape, MASK_VALUE, F32)
        l_ref[...] = jnp.zeros(l_ref.shape, F32)
        acc_ref[...] = jnp.zeros(acc_ref.shape, F32)

    def attend(c_list, r_list, valid):
        s = jnp.concatenate(
            [lax.dot_general(qa, c, _NT, preferred_element_type=F32)
             + jnp.dot(qr, r, preferred_element_type=F32) for c, r in zip(c_list, r_list)], axis=1)
        s = s * scale
        if valid is not None:
            s = jnp.where(valid, s, MASK_VALUE)
        _online_softmax_step(s, valid, c_list, m_ref, l_ref, acc_ref)

    attend([r[0, 0].astype(BF16) for r in c_refs], [r[0, 0].astype(BF16) for r in r_refs], None)

    @pl.when(gi == pl.num_programs(1) - 1)
    def _():
        rows = qa_ref.shape[1]
        tok = lax.broadcasted_iota(jnp.int32, (rows, PAGE_SIZE), 0) & (t - 1)
        col = lax.broadcasted_iota(jnp.int32, (rows, PAGE_SIZE), 1)
        attend([cn_ref[0].astype(BF16)], [rn_ref[0].astype(BF16)], col <= tok)
        o_lat = (acc_ref[...] / l_ref[...]).astype(BF16)
        for h in range(n_heads):
            o_ref[0, h] = jnp.dot(o_lat[h * t:(h + 1) * t], wuv_ref[h].astype(BF16), preferred_element_type=F32)


def sample_mla_attention(q_abs, q_rope, c_new, r_new, w_uv, pool_ckv, pool_ckr, layer, page_table, pg, scale):
    db, t, nh, dl = q_abs.shape
    assert t & (t - 1) == 0
    dr = q_rope.shape[3]
    dv = w_uv.shape[2]
    n_pages = page_table.shape[1]
    rows = nh * t
    qa = q_abs.transpose(0, 2, 1, 3).reshape(db, rows, dl)
    qr = q_rope.transpose(0, 2, 1, 3).reshape(db, rows, dr)
    cn = jnp.pad(c_new, ((0, 0), (0, PAGE_SIZE - t), (0, 0)))
    rn = _tokens_last(r_new)
    grid_spec = pltpu.PrefetchScalarGridSpec(
        num_scalar_prefetch=1,
        grid=(db, n_pages // pg),
        in_specs=[pl.BlockSpec((1, rows, dl), lambda b, g, pt: (b, 0, 0)),
                  pl.BlockSpec((1, rows, dr), lambda b, g, pt: (b, 0, 0)),
                  pl.BlockSpec((1, PAGE_SIZE, dl), lambda b, g, pt: (b, 0, 0)),
                  pl.BlockSpec((1, dr, PAGE_SIZE), lambda b, g, pt: (b, 0, 0)),
                  pl.BlockSpec((nh, dl, dv), lambda b, g, pt: (0, 0, 0))]
                 + _page_specs(layer, n_pages, pg, PAGE_SIZE, dl) + _page_specs(layer, n_pages, pg, dr, PAGE_SIZE),
        out_specs=pl.BlockSpec((1, nh, t, dv), lambda b, g, pt: (b, 0, 0, 0)),
        scratch_shapes=[pltpu.VMEM((rows, 1), F32), pltpu.VMEM((rows, 1), F32), pltpu.VMEM((rows, dl), F32)],
    )
    out = pl.pallas_call(
        functools.partial(_smla_kernel, pg=pg, scale=scale),
        grid_spec=grid_spec,
        out_shape=jax.ShapeDtypeStruct((db, nh, t, dv), F32),
        compiler_params=pltpu.CompilerParams(dimension_semantics=("parallel", "arbitrary"),
                                             vmem_limit_bytes=VMEM_LIMIT),
        name="sample_mla_attention",
    )(page_table.reshape(-1), qa, qr, cn, rn, w_uv.transpose(1, 0, 2), *([pool_ckv] * pg), *([pool_ckr] * pg))
    return out.transpose(0, 2, 1, 3).reshape(db, t, nh * dv)


def mla_attend(q_abs, q_rope, c_kv, k_rope, q_pos, k_pos, w_uv):
    b, t = q_abs.shape[:2]
    s = jnp.einsum('bthr,bsr->bhts', q_abs, c_kv) + jnp.einsum('bthp,bsp->bhts', q_rope, k_rope)
    s = s * ((C_NOPE + C_ROPE) ** -0.5)
    s = jnp.where(k_pos[None, None, None, :] <= q_pos[None, None, :, None], s, MASK_VALUE)
    p = jax.nn.softmax(s, axis=-1)
    o_lat = jnp.einsum('bhts,bsr->bthr', p, c_kv)
    return jnp.einsum('bthr,rhv->bthv', o_lat, w_uv).reshape(b, t, C_HEADS * C_V)


def _pmla_kernel(q_ref, k_ref, wuv_ref, o_ref, m_ref, l_ref, acc_ref, *, tk, dl, scale):
    n_heads, tq, dq = q_ref.shape[1:]
    rows = n_heads * tq
    qb = pl.program_id(1)
    q = q_ref[0].reshape(rows, dq)
    m_ref[...] = jnp.full(m_ref.shape, MASK_VALUE, F32)
    l_ref[...] = jnp.zeros(l_ref.shape, F32)
    acc_ref[...] = jnp.zeros(acc_ref.shape, F32)
    qpos = qb * tq + (lax.broadcasted_iota(jnp.int32, (rows, tk), 0) & (tq - 1))
    kofs = lax.broadcasted_iota(jnp.int32, (rows, tk), 1)

    def body(c, carry):
        start = pl.multiple_of(c * tk, tk)
        kc = k_ref[0, pl.ds(start, tk), :]
        s = lax.dot_general(q, kc, _NT, preferred_element_type=F32) * scale
        s = jnp.where(start + kofs <= qpos, s, MASK_VALUE)
        m_old = m_ref[...]
        m_new = jnp.maximum(m_old, jnp.max(s, axis=1, keepdims=True))
        alpha = jnp.exp(m_old - m_new)
        p = jnp.exp(s - m_new)
        l_ref[...] = alpha * l_ref[...] + jnp.sum(p, axis=1, keepdims=True)
        acc_ref[...] = alpha * acc_ref[...] + jnp.dot(p.astype(BF16), kc[:, :dl], preferred_element_type=F32)
        m_ref[...] = m_new
        return carry

    lax.fori_loop(0, (qb * tq) // tk + 1, body, 0)
    o_lat = (acc_ref[...] / l_ref[...]).astype(BF16)
    for h in range(n_heads):
        o_ref[0, h] = jnp.dot(o_lat[h * tq:(h + 1) * tq], wuv_ref[h].astype(BF16), preferred_element_type=F32)


def mla_prompt(q_abs, q_rope, c_kv, k_rope, w_uv, tq=128, tk=512):
    b, l, nh, dl = q_abs.shape
    dr, dv = q_rope.shape[3], w_uv.shape[2]
    tk = min(tk, l)
    assert tq & (tq - 1) == 0 and l % tk == 0 and tk % tq == 0
    qcat = jnp.concatenate([q_abs, q_rope], -1).transpose(0, 2, 1, 3).astype(BF16)
    kcat = jnp.concatenate([c_kv, k_rope], -1).astype(BF16)
    out = pl.pallas_call(
        functools.partial(_pmla_kernel, tk=tk, dl=dl, scale=(C_NOPE + C_ROPE) ** -0.5),
        grid=(b, l // tq),
        in_specs=[pl.BlockSpec((1, nh, tq, dl + dr), lambda bi, qb: (bi, 0, qb, 0)),
                  pl.BlockSpec((1, l, dl + dr), lambda bi, qb: (bi, 0, 0)),
                  pl.BlockSpec((nh, dl, dv), lambda bi, qb: (0, 0, 0))],
        out_specs=pl.BlockSpec((1, nh, tq, dv), lambda bi, qb: (bi, 0, qb, 0)),
        out_shape=jax.ShapeDtypeStruct((b, nh, l, dv), F32),
        scratch_shapes=[pltpu.VMEM((nh * tq, 1), F32), pltpu.VMEM((nh * tq, 1), F32),
                        pltpu.VMEM((nh * tq, dl), F32)],
        compiler_params=pltpu.CompilerParams(dimension_semantics=("parallel", "arbitrary"),
                                             vmem_limit_bytes=VMEM_LIMIT),
        name="mla_prompt",
    )(qcat, kcat, w_uv.transpose(1, 0, 2))
    return out.transpose(0, 2, 1, 3).reshape(b, l, nh * dv)


def prompt_mixers(h, pos, w_in, lb, a_norm_g, cq_g, w_uq, ckv_g, w_uk, w_uv):
    (a_q, a_logf, a_k, a_v, a_g), (b_q, b_k, b_v, b_qi, b_ki, b_w), (c_qn, c_qr, c_kv, c_kr) = project(
        h, w_in, pos, lb, cq_g, w_uq, ckv_g)
    s0 = jnp.zeros((h.shape[0], A_HEADS, A_DK, A_DV), F32)
    o_a, s_a = hgrn2_recurrence(a_q, a_logf, a_k, a_v, s0)
    o_a = hgrn_output(o_a, a_g, a_norm_g)
    o_b = dsa_prompt(b_q, b_k, b_v, b_qi, b_ki, b_w, pos)
    q_abs = jnp.einsum('bthn,rhn->bthr', c_qn, w_uk)
    o_c = mla_prompt(q_abs, c_qr, c_kv, c_kr, w_uv)
    mix = jnp.concatenate([o_a, o_b, o_c], -1)
    return mix, (b_k, b_v, b_ki, c_kv, c_kr, s_a)


def sample_mixers(h, pos, past, layer, pool_bk, pool_bv, pool_bidx, pool_ckv, pool_ckr, s_prev, page_table,
                  w_in, lb, a_norm_g, cq_g, w_uq, ckv_g, w_uk, w_uv):
    (a_q, a_logf, a_k, a_v, a_g), (b_q, b_k, b_v, b_qi, b_ki, b_w), (c_qn, c_qr, c_kv, c_kr) = project(
        h, w_in, pos, lb, cq_g, w_uq, ckv_g)
    o_a, s_a = hgrn2_recurrence(a_q, a_logf, a_k, a_v, s_prev)
    o_a = hgrn_output(o_a, a_g, a_norm_g)
    o_b = dsa_sample(b_q, b_k, b_v, b_qi, b_ki, b_w, past, pool_bk, pool_bv, pool_bidx, layer, page_table)
    q_abs = jnp.einsum('bthn,rhn->bthr', c_qn, w_uk)
    o_c = sample_mla_attention(q_abs, c_qr, c_kv, c_kr, w_uv, pool_ckv, pool_ckr, layer, page_table,
                               PAGES_PER_STEP, (C_NOPE + C_ROPE) ** -0.5)
    mix = jnp.concatenate([o_a, o_b, o_c], -1)
    return mix, (b_k, b_v, b_ki, c_kv, c_kr, s_a)


def finish_layer(x, mix, w_o, ln1_g, ln1_b, w_gate, w_up, w_down, ln2_g, ln2_b):
    b, l, d = x.shape
    x1, x1b = matmul_res_ln(mix.reshape(b * l, d).astype(BF16), w_o, x.reshape(b * l, d), ln1_g, ln1_b)
    hmid = ffn_up(x1b, w_gate, w_up)
    x2, x2b = matmul_res_ln(hmid, w_down, x1, ln2_g, ln2_b)
    return x2.reshape(b, l, d), x2b.reshape(b, l, d)


def kernel(x_prompt, x_sample, cache_b_k, cache_b_v, cache_b_idx, cache_c_kv, cache_c_kr, state_a,
           page_table, ln_in_g, ln_in_b, w_in, hgrn_lb_logits, hgrn_norm_g, mla_q_norm_g, mla_w_uq,
           mla_kv_norm_g, mla_w_uk, mla_w_uv, w_o, ln1_g, ln1_b, w_gate, w_up, w_down, ln2_g, ln2_b):
    lower_bounds = hgrn_lower_bounds(hgrn_lb_logits)
    past = page_table.shape[1] * PAGE_SIZE
    pos_p = jnp.arange(x_prompt.shape[1], dtype=jnp.int32)
    pos_s = past + jnp.arange(x_sample.shape[1], dtype=jnp.int32)
    xp = layer_norm(x_prompt, ln_in_g, ln_in_b)
    xs = layer_norm(x_sample, ln_in_g, ln_in_b)
    kv_width = B_KV_HEADS * B_HEAD_DIM
    pool_bk = cache_b_k.transpose(0, 1, 3, 4, 2).reshape(cache_b_k.shape[:2] + (kv_width, PAGE_SIZE))
    pool_bv = cache_b_v.transpose(0, 1, 3, 4, 2).reshape(cache_b_v.shape[:2] + (kv_width, PAGE_SIZE))
    pool_bidx = cache_b_idx.transpose(0, 1, 3, 2)
    pool_ckr = cache_c_kr.transpose(0, 1, 3, 2)
    w_in_b, w_o_b, w_gate_b, w_up_b, w_down_b = (w.astype(BF16) for w in (w_in, w_o, w_gate, w_up, w_down))
    xpb, xsb = xp.astype(BF16), xs.astype(BF16)
    new_p = ([], [], [], [], [], [])
    new_s = ([], [], [], [], [], [])
    for l in range(DEPTH):
        mix_w = (w_in_b[l], lower_bounds[l], hgrn_norm_g[l], mla_q_norm_g[l], mla_w_uq[l],
                 mla_kv_norm_g[l], mla_w_uk[l], mla_w_uv[l])
        ffn_w = (w_o_b[l], ln1_g[l], ln1_b[l], w_gate_b[l], w_up_b[l], w_down_b[l], ln2_g[l], ln2_b[l])
        mix_p, st_p = prompt_mixers(xpb, pos_p, *mix_w)
        mix_s, st_s = sample_mixers(xsb, pos_s, past, l, pool_bk, pool_bv, pool_bidx,
                                    cache_c_kv, pool_ckr, state_a[l], page_table, *mix_w)
        xp, xpb = finish_layer(xp, mix_p, *ffn_w)
        xs, xsb = finish_layer(xs, mix_s, *ffn_w)
        for lst, st in zip(new_p, st_p):
            lst.append(st)
        for lst, st in zip(new_s, st_s):
            lst.append(st)
    b_k_p, b_v_p, b_idx_p, c_kv_p, c_kr_p, s_a_p = [jnp.stack(v, 0) for v in new_p]
    b_k_s, b_v_s, b_idx_s, c_kv_s, c_kr_s, s_a_s = [jnp.stack(v, 0) for v in new_s]
    return (xp, xs, b_k_p, b_v_p, b_idx_p, c_kv_p, c_kr_p, s_a_p,
            b_k_s, b_v_s, b_idx_s, c_kv_s, c_kr_s, s_a_s)
```

```python
import functools
import math

import jax
import jax.numpy as jnp
import numpy as np
from jax import lax
from jax.experimental import pallas as pl
from jax.experimental.pallas import tpu as pltpu

D_MODEL = 1024
DEPTH = 4
PAGE_SIZE = 128
A_HEADS = 4
A_DK = 128
A_DV = 128
A_CHUNK = 32
B_HEADS = 4
B_KV_HEADS = 2
B_HEAD_DIM = 64
IDX_HEADS = 8
IDX_DIM = 64
TOPK_MAX = 256
C_HEADS = 4
C_NOPE = 64
C_ROPE = 32
C_V = 64
C_Q_LORA = 256
C_KV_LORA = 128
ROPE_THETA = 10000.0
Q_BLOCK = 128
EPS = 1e-5
MASK_VALUE = -1e30
DEEPNORM_ALPHA = (2 * DEPTH) ** 0.25
IN_SPLITS = (A_HEADS * A_DK, A_HEADS * A_DK, A_HEADS * A_DV, A_HEADS * A_DV,
             B_HEADS * B_HEAD_DIM, B_KV_HEADS * B_HEAD_DIM, B_KV_HEADS * B_HEAD_DIM,
             IDX_HEADS * IDX_DIM, IDX_DIM, IDX_HEADS,
             C_Q_LORA, C_KV_LORA, C_ROPE)
F32 = jnp.float32
BF16 = jnp.bfloat16


def _mm_kernel(x_ref, w_ref, o_ref):
    o_ref[...] = jnp.dot(x_ref[...], w_ref[...], preferred_element_type=F32)


def _matmul(x, w, tm=1024, tn=512):
    m, k = x.shape
    n = w.shape[1]
    tm = min(tm, m)
    tn = min(tn, n)
    return pl.pallas_call(
        _mm_kernel,
        grid=(pl.cdiv(m, tm), pl.cdiv(n, tn)),
        in_specs=[pl.BlockSpec((tm, k), lambda i, j: (i, 0)),
                  pl.BlockSpec((k, tn), lambda i, j: (0, j))],
        out_specs=pl.BlockSpec((tm, tn), lambda i, j: (i, j)),
        out_shape=jax.ShapeDtypeStruct((m, n), F32),
        compiler_params=pltpu.CompilerParams(dimension_semantics=("parallel", "parallel"),
                                             vmem_limit_bytes=48 * 1024 * 1024),
        name="dense_matmul",
    )(x, w)


def _mm3(h, w):
    b, l, d = h.shape
    return _matmul(h.reshape(b * l, d), w).reshape(b, l, w.shape[1])


def _ffn_up_kernel(x_ref, wg_ref, wu_ref, h_ref):
    x = x_ref[...]
    g = jnp.dot(x, wg_ref[...], preferred_element_type=F32)
    u = jnp.dot(x, wu_ref[...], preferred_element_type=F32)
    h_ref[...] = (g * jax.nn.sigmoid(g) * u).astype(BF16)


def ffn_up(x, w_gate, w_up, tm=512):
    m, d = x.shape
    f = w_gate.shape[1]
    tm = min(tm, m)
    tn = f // 2
    assert m % tm == 0 and tn % 128 == 0
    return pl.pallas_call(
        _ffn_up_kernel,
        grid=(m // tm, 2),
        in_specs=[pl.BlockSpec((tm, d), lambda i, j: (i, 0)),
                  pl.BlockSpec((d, tn), lambda i, j: (0, j)),
                  pl.BlockSpec((d, tn), lambda i, j: (0, j))],
        out_specs=pl.BlockSpec((tm, tn), lambda i, j: (i, j)),
        out_shape=jax.ShapeDtypeStruct((m, f), BF16),
        compiler_params=pltpu.CompilerParams(dimension_semantics=("parallel", "parallel"),
                                             vmem_limit_bytes=48 * 1024 * 1024),
        name="ffn_up",
    )(x, w_gate, w_up)


def _mm_res_ln_kernel(a_ref, w_ref, res_ref, g_ref, b_ref, o_ref, ob_ref):
    y = DEEPNORM_ALPHA * res_ref[...] + jnp.dot(a_ref[...], w_ref[...], preferred_element_type=F32)
    mu = jnp.mean(y, axis=-1, keepdims=True)
    var = jnp.mean(jnp.square(y - mu), axis=-1, keepdims=True)
    out = (y - mu) * lax.rsqrt(var + EPS) * g_ref[...] + b_ref[...]
    o_ref[...] = out
    ob_ref[...] = out.astype(BF16)


def matmul_res_ln(a, w, res, g, b, tm=512):
    m, k = a.shape
    d = w.shape[1]
    tm = min(tm, m)
    assert m % tm == 0
    return pl.pallas_call(
        _mm_res_ln_kernel,
        grid=(m // tm,),
        in_specs=[pl.BlockSpec((tm, k), lambda i: (i, 0)),
                  pl.BlockSpec((k, d), lambda i: (0, 0)),
                  pl.BlockSpec((tm, d), lambda i: (i, 0)),
                  pl.BlockSpec((1, d), lambda i: (0, 0)),
                  pl.BlockSpec((1, d), lambda i: (0, 0))],
        out_specs=[pl.BlockSpec((tm, d), lambda i: (i, 0)), pl.BlockSpec((tm, d), lambda i: (i, 0))],
        out_shape=[jax.ShapeDtypeStruct((m, d), F32), jax.ShapeDtypeStruct((m, d), BF16)],
        compiler_params=pltpu.CompilerParams(dimension_semantics=("parallel",),
                                             vmem_limit_bytes=48 * 1024 * 1024),
        name="matmul_res_ln",
    )(a, w, res, g.reshape(1, d), b.reshape(1, d))


def layer_norm(x, g, b):
    mu = jnp.mean(x, -1, keepdims=True)
    var = jnp.mean(jnp.square(x - mu), -1, keepdims=True)
    return (x - mu) * lax.rsqrt(var + EPS) * g + b


def rms_norm(x, g):
    return x * lax.rsqrt(jnp.mean(jnp.square(x), -1, keepdims=True) + EPS) * g


def rope(x, pos):
    half = x.shape[-1] // 2
    inv = ROPE_THETA ** (-jnp.arange(half, dtype=F32) / half)
    ang = pos.astype(F32)[:, None] * inv[None, :]
    shape = (ang.shape[0],) + (1,) * (x.ndim - 3) + (half,)
    cos = jnp.cos(ang).reshape(shape)
    sin = jnp.sin(ang).reshape(shape)
    x1, x2 = x[..., :half], x[..., half:]
    return jnp.concatenate([x1 * cos - x2 * sin, x2 * cos + x1 * sin], -1)


def hgrn_lower_bounds(logits):
    p = jax.nn.softmax(logits.astype(F32), axis=0)
    return jnp.clip(jnp.cumsum(p, axis=0) - p[0:1], 0.0, 1.0 - 1e-4)


def to_blocks(t, qb):
    b, l = t.shape[:2]
    return t.reshape((b, l // qb, qb) + t.shape[2:]).swapaxes(0, 1)


def from_blocks(t):
    nb, b, qb = t.shape[:3]
    return t.swapaxes(0, 1).reshape((b, nb * qb) + t.shape[3:])


def project(h, w_in, pos, lb, cq_g, w_uq, ckv_g):
    b, l = h.shape[:2]
    offs = [int(o) for o in np.cumsum(IN_SPLITS)[:-1]]
    aq, af, ai, ag, bq, bk, bv, bqi, bki, bw, cdq, cdkv, ckr = jnp.split(_mm3(h, w_in), offs, axis=-1)
    z = af.reshape(b, l, A_HEADS, A_DK)
    lbh = lb.reshape(A_HEADS, A_DK)
    f = lbh + (1.0 - lbh) * jax.nn.sigmoid(z)
    a_logf = jnp.log(jnp.maximum(f, 1e-30))
    a_k = 1.0 - f
    a_q = aq.reshape(b, l, A_HEADS, A_DK)
    a_v = ai.reshape(b, l, A_HEADS, A_DV)
    b_q = rope(bq.reshape(b, l, B_HEADS, B_HEAD_DIM), pos)
    b_k = rope(bk.reshape(b, l, B_KV_HEADS, B_HEAD_DIM), pos)
    b_v = bv.reshape(b, l, B_KV_HEADS, B_HEAD_DIM)
    b_qi = rope(bqi.reshape(b, l, IDX_HEADS, IDX_DIM), pos)
    b_ki = rope(bki[:, :, None, :], pos)[:, :, 0]
    b_w = bw * (IDX_HEADS ** -0.5 * IDX_DIM ** -0.5)
    cq = jnp.einsum('blr,rhd->blhd', rms_norm(cdq, cq_g), w_uq)
    c_qn = cq[..., :C_NOPE]
    c_qr = rope(cq[..., C_NOPE:], pos)
    c_kv = rms_norm(cdkv, ckv_g)
    c_kr = rope(ckr[:, :, None, :], pos)[:, :, 0]
    return (a_q, a_logf, a_k, a_v, ag), (b_q, b_k, b_v, b_qi, b_ki, b_w), (c_qn, c_qr, c_kv, c_kr)


def hgrn2_recurrence(q, log_f, k, v, s0):
    b, l, h, _ = q.shape
    c = math.gcd(l, A_CHUNK)
    n = l // c

    def blocks(t):
        return t.reshape(b, n, c, h, t.shape[-1]).transpose(1, 0, 3, 2, 4)

    causal = jnp.tril(jnp.ones((c, c), bool))

    def step(s, inp):
        qc, lf, kc, vc = inp
        cum = jnp.cumsum(lf, axis=2)
        diff = cum[:, :, :, None, :] - cum[:, :, None, :, :]
        decay = jnp.exp(jnp.where(causal[:, :, None], diff, MASK_VALUE))
        attn = jnp.einsum('bhtd,bhtsd,bhsd->bhts', qc, decay, kc)
        o = attn @ vc + jnp.einsum('bhtd,bhdv->bhtv', qc * jnp.exp(cum), s)
        last = cum[:, :, -1:, :]
        s_new = jnp.exp(last)[:, :, 0, :, None] * s + jnp.einsum('bhsd,bhsv->bhdv', kc * jnp.exp(last - cum), vc)
        return s_new, o

    s_fin, o = lax.scan(step, s0, (blocks(q), blocks(log_f), blocks(k), blocks(v)))
    return o.transpose(1, 0, 3, 2, 4).reshape(b, l, h, -1), s_fin


def hgrn_output(o, g, norm_g):
    b, l = o.shape[:2]
    return rms_norm(o, norm_g).reshape(b, l, -1) * jax.nn.silu(g)


def indexer_scores(qi, w, ki, q_pos, k_pos):
    score = jnp.zeros(qi.shape[:2] + (ki.shape[1],), F32)
    for hh in range(IDX_HEADS):
        dots = jnp.einsum('btd,bsd->bts', qi[:, :, hh], ki)
        score = score + w[:, :, hh, None] * jax.nn.relu(dots)
    return jnp.where(k_pos[None, None, :] <= q_pos[None, :, None], score, MASK_VALUE)


def sparse_attention(q, kg, vg, valid):
    b, t = q.shape[:2]
    qg = q.reshape(b, t, B_KV_HEADS, B_HEADS // B_KV_HEADS, B_HEAD_DIM)
    s = jnp.einsum('btkgd,btjkd->btkgj', qg, kg) * (B_HEAD_DIM ** -0.5)
    s = jnp.where(valid[:, :, None, None, :], s, MASK_VALUE)
    p = jax.nn.softmax(s, axis=-1)
    return jnp.einsum('btkgj,btjkd->btkgd', p, vg).reshape(b, t, B_HEADS * B_HEAD_DIM)


def gather_rows(rows, idx):
    return jax.vmap(lambda r, i: r[i])(rows, idx)


INT_MIN = -2 ** 31
_NT = (((1,), (1,)), ((), ()))


REDUCE_LANES = 64
KEY_STEP = 512


def _reduce_rows(x, op):
    rows = x.shape[0]
    if rows > REDUCE_LANES and rows % REDUCE_LANES == 0:
        x = op(x.reshape(rows // REDUCE_LANES, REDUCE_LANES, x.shape[1]), axis=0)
    return op(x, axis=0, keepdims=True)


def _dsa_prompt_body(lk, qi_ref, wT_ref, ki_ref, q_ref, k_ref, vT_ref, o_ref, key_ref, sel_ref, *, topk, scale):
    n_idx, tq = qi_ref.shape[1], qi_ref.shape[2]
    qb = pl.program_id(1)
    ki = ki_ref[0, :lk].astype(BF16)
    acc = jnp.zeros((lk, tq), F32)
    for h in range(n_idx):
        d = lax.dot_general(ki, qi_ref[0, h].astype(BF16), _NT, preferred_element_type=F32)
        acc = acc + wT_ref[0, h:h + 1, :] * jnp.maximum(d, 0.0)
    kpos = lax.broadcasted_iota(jnp.int32, (lk, tq), 0)
    qpos = qb * tq + lax.broadcasted_iota(jnp.int32, (lk, tq), 1)
    causal = kpos <= qpos
    s = jnp.where(causal, acc, MASK_VALUE)
    s = jnp.where(s == 0.0, 0.0, s)
    bits = pltpu.bitcast(s, jnp.int32)
    key_ref[:lk] = bits ^ ((bits >> 31) & 0x7FFFFFFF)

    def count_ge(cand):
        return _reduce_rows(jnp.where(key_ref[:lk] >= cand, 1.0, 0.0), jnp.sum)

    kf = float(topk)
    t0 = jnp.where(count_ge(jnp.zeros((1, tq), jnp.int32)) >= kf, 0, INT_MIN).astype(jnp.int32)

    def step(i, t):
        cand = t | jnp.left_shift(jnp.int32(1), 30 - i)
        return jnp.where(count_ge(cand) >= kf, cand, t)

    thr = lax.fori_loop(0, 31, step, t0)
    need = kf - _reduce_rows(jnp.where(key_ref[:lk] > thr, 1.0, 0.0), jnp.sum)
    tri = (lax.broadcasted_iota(jnp.int32, (128, 128), 0) >= lax.broadcasted_iota(jnp.int32, (128, 128), 1))
    tri = jnp.where(tri, 1.0, 0.0).astype(BF16)
    offset = jnp.zeros((1, tq), F32)
    for c in range(lk // 128):
        key_c = key_ref[c * 128:(c + 1) * 128, :]
        tie = key_c == thr
        pref = jnp.dot(tri, jnp.where(tie, 1.0, 0.0).astype(BF16), preferred_element_type=F32) + offset
        offset = pref[127:128, :]
        chosen = jnp.where(key_c > thr, 1.0, jnp.where(tie, jnp.where(pref <= need, 1.0, 0.0), 0.0))
        sel_ref[c * 128:(c + 1) * 128, :] = jnp.where(causal[c * 128:(c + 1) * 128, :], chosen, 0.0)

    n_kv = k_ref.shape[1]
    grp = q_ref.shape[1] // n_kv
    for j in range(n_kv):
        kj = k_ref[0, j, :lk].astype(BF16)
        vTj = vT_ref[0, j, :, :lk].astype(BF16)
        for g in range(grp):
            h = j * grp + g
            sT = lax.dot_general(kj, q_ref[0, h].astype(BF16), _NT, preferred_element_type=F32) * scale
            sT = jnp.where(sel_ref[:lk] > 0.0, sT, MASK_VALUE)
            m = _reduce_rows(sT, jnp.max)
            p = jnp.exp(sT - m)
            l = _reduce_rows(p, jnp.sum)
            oT = jnp.dot(vTj, p.astype(BF16), preferred_element_type=F32)
            o_ref[0, h] = oT / l


def _dsa_prompt_kernel(qi_ref, wT_ref, ki_ref, *rest, topk, scale):
    tq, L = qi_ref.shape[2], ki_ref.shape[1]
    step = min(KEY_STEP, L)
    assert L % step == 0 and step % tq == 0 and step >= topk
    extent = (pl.program_id(1) * tq) // step
    for i in range(L // step):
        pl.when(extent == i)(functools.partial(_dsa_prompt_body, (i + 1) * step, qi_ref, wT_ref, ki_ref, *rest,
                                               topk=topk, scale=scale))


def dsa_prompt(q, k, v, qi, ki, w, pos, tq=128):
    b, l, nh, dh = q.shape
    n_idx, n_kv = qi.shape[2], k.shape[2]
    topk = min(TOPK_MAX, l // 4)
    out = pl.pallas_call(
        functools.partial(_dsa_prompt_kernel, topk=topk, scale=dh ** -0.5),
        grid=(b, l // tq),
        in_specs=[
            pl.BlockSpec((1, n_idx, tq, qi.shape[3]), lambda bi, qb: (bi, 0, qb, 0)),
            pl.BlockSpec((1, n_idx, tq), lambda bi, qb: (bi, 0, qb)),
            pl.BlockSpec((1, l, ki.shape[2]), lambda bi, qb: (bi, 0, 0)),
            pl.BlockSpec((1, nh, tq, dh), lambda bi, qb: (bi, 0, qb, 0)),
            pl.BlockSpec((1, n_kv, l, dh), lambda bi, qb: (bi, 0, 0, 0)),
            pl.BlockSpec((1, n_kv, dh, l), lambda bi, qb: (bi, 0, 0, 0)),
        ],
        out_specs=pl.BlockSpec((1, nh, dh, tq), lambda bi, qb: (bi, 0, 0, qb)),
        out_shape=jax.ShapeDtypeStruct((b, nh, dh, l), F32),
        scratch_shapes=[pltpu.VMEM((l, tq), jnp.int32), pltpu.VMEM((l, tq), F32)],
        compiler_params=pltpu.CompilerParams(dimension_semantics=("parallel", "arbitrary"),
                                             vmem_limit_bytes=48 * 1024 * 1024),
        name="dsa_prompt",
    )(qi.transpose(0, 2, 1, 3), w.transpose(0, 2, 1), ki, q.transpose(0, 2, 1, 3),
      k.transpose(0, 2, 1, 3), v.transpose(0, 2, 3, 1))
    return out.transpose(0, 3, 1, 2).reshape(b, l, nh * dh)


VMEM_LIMIT = 48 * 1024 * 1024
PAGES_PER_STEP = 32


def _page_specs(layer, n_pages, pg, rows, cols):
    def spec(j):
        return pl.BlockSpec((1, 1, rows, cols),
                            lambda b, g, pt: (layer, pt[b * n_pages + g * pg + j], 0, 0))
    return [spec(j) for j in range(pg)]


def _tokens_last(x_new):
    t = x_new.shape[1]
    return jnp.pad(x_new.transpose(0, 2, 1), ((0, 0), (0, 0), (0, PAGE_SIZE - t)))


def _sidx_kernel(pt_ref, qi_ref, w_ref, kn_ref, *rest, pg):
    page_refs, (sc_ref, scn_ref) = rest[:pg], rest[pg:]
    n_tok = sc_ref.shape[1]
    qi = qi_ref[0].astype(BF16)
    w = w_ref[0]

    def score(keys_t):
        d = jnp.dot(qi, keys_t.astype(BF16), preferred_element_type=F32)
        r = w * jnp.maximum(d, 0.0)
        acc = jnp.zeros((n_tok, PAGE_SIZE), F32)
        for h in range(r.shape[0] // n_tok):
            acc = acc + r[h * n_tok:(h + 1) * n_tok]
        return acc

    for j in range(pg):
        sc_ref[0, :, j * PAGE_SIZE:(j + 1) * PAGE_SIZE] = score(page_refs[j][0, 0])
    row = lax.broadcasted_iota(jnp.int32, (n_tok, PAGE_SIZE), 0)
    col = lax.broadcasted_iota(jnp.int32, (n_tok, PAGE_SIZE), 1)
    scn_ref[0] = jnp.where(col <= row, score(kn_ref[0]), MASK_VALUE)


def sample_indexer_scores(qi, w, ki_new, pool_idx, layer, page_table, pg):
    db, t, nh, d = qi.shape
    n_pages = page_table.shape[1]
    qi_s = qi.transpose(0, 2, 1, 3).reshape(db, nh * t, d)
    w_s = w.transpose(0, 2, 1).reshape(db, nh * t, 1)
    kn = _tokens_last(ki_new)
    grid_spec = pltpu.PrefetchScalarGridSpec(
        num_scalar_prefetch=1,
        grid=(db, n_pages // pg),
        in_specs=[pl.BlockSpec((1, nh * t, d), lambda b, g, pt: (b, 0, 0)),
                  pl.BlockSpec((1, nh * t, 1), lambda b, g, pt: (b, 0, 0)),
                  pl.BlockSpec((1, d, PAGE_SIZE), lambda b, g, pt: (b, 0, 0))]
                 + _page_specs(layer, n_pages, pg, d, PAGE_SIZE),
        out_specs=[pl.BlockSpec((1, t, pg * PAGE_SIZE), lambda b, g, pt: (b, 0, g)),
                   pl.BlockSpec((1, t, PAGE_SIZE), lambda b, g, pt: (b, 0, 0))],
    )
    return pl.pallas_call(
        functools.partial(_sidx_kernel, pg=pg),
        grid_spec=grid_spec,
        out_shape=[jax.ShapeDtypeStruct((db, t, n_pages * PAGE_SIZE), F32),
                   jax.ShapeDtypeStruct((db, t, PAGE_SIZE), F32)],
        compiler_params=pltpu.CompilerParams(dimension_semantics=("parallel", "arbitrary"),
                                             vmem_limit_bytes=VMEM_LIMIT),
        name="sample_indexer",
    )(page_table.reshape(-1), qi_s, w_s, kn, *([pool_idx] * pg))


def _ssel_kernel(sc_ref, scn_ref, sel_ref, seln_ref, key_ref, *, topk):
    g, t, p = sc_ref.shape
    rows = g * t
    n_chunks = p // PAGE_SIZE + 1

    def to_key(s):
        s = jnp.where(s == 0.0, 0.0, s)
        bits = pltpu.bitcast(s, jnp.int32)
        return bits ^ ((bits >> 31) & 0x7FFFFFFF)

    key_ref[:, :p] = to_key(sc_ref[...].reshape(rows, p))
    key_ref[:, p:] = to_key(scn_ref[...].reshape(rows, PAGE_SIZE))

    def count_ge(cand):
        return jnp.sum(jnp.where(key_ref[...] >= cand, 1.0, 0.0), axis=1, keepdims=True)

    kf = float(topk)
    t0 = jnp.where(count_ge(jnp.zeros((rows, 1), jnp.int32)) >= kf, 0, INT_MIN).astype(jnp.int32)

    def step(i, thr):
        cand = thr | jnp.left_shift(jnp.int32(1), 30 - i)
        return jnp.where(count_ge(cand) >= kf, cand, thr)

    thr = lax.fori_loop(0, 31, step, t0)
    need = kf - jnp.sum(jnp.where(key_ref[...] > thr, 1.0, 0.0), axis=1, keepdims=True)
    tri = (lax.broadcasted_iota(jnp.int32, (PAGE_SIZE, PAGE_SIZE), 0)
           <= lax.broadcasted_iota(jnp.int32, (PAGE_SIZE, PAGE_SIZE), 1))
    tri = jnp.where(tri, 1.0, 0.0).astype(BF16)
    offset = jnp.zeros((rows, 1), F32)
    trow = lax.broadcasted_iota(jnp.int32, (rows, PAGE_SIZE), 0) & (t - 1)
    tcol = lax.broadcasted_iota(jnp.int32, (rows, PAGE_SIZE), 1)
    for c in range(n_chunks):
        key_c = key_ref[:, c * PAGE_SIZE:(c + 1) * PAGE_SIZE]
        tie = key_c == thr
        pref = jnp.dot(jnp.where(tie, 1.0, 0.0).astype(BF16), tri, preferred_element_type=F32) + offset
        offset = pref[:, PAGE_SIZE - 1:PAGE_SIZE]
        chosen = jnp.where(key_c > thr, 1.0, jnp.where(tie, jnp.where(pref <= need, 1.0, 0.0), 0.0))
        if c < n_chunks - 1:
            sel_ref[:, :, c * PAGE_SIZE:(c + 1) * PAGE_SIZE] = chosen.reshape(g, t, PAGE_SIZE)
        else:
            seln_ref[...] = jnp.where(tcol <= trow, chosen, 0.0).reshape(g, t, PAGE_SIZE)


def sample_select(sc, scn, topk, g=8):
    db, t, p = sc.shape
    assert t & (t - 1) == 0 and db % g == 0
    return pl.pallas_call(
        functools.partial(_ssel_kernel, topk=topk),
        grid=(db // g,),
        in_specs=[pl.BlockSpec((g, t, p), lambda i: (i, 0, 0)),
                  pl.BlockSpec((g, t, PAGE_SIZE), lambda i: (i, 0, 0))],
        out_specs=[pl.BlockSpec((g, t, p), lambda i: (i, 0, 0)),
                   pl.BlockSpec((g, t, PAGE_SIZE), lambda i: (i, 0, 0))],
        out_shape=[jax.ShapeDtypeStruct((db, t, p), F32), jax.ShapeDtypeStruct((db, t, PAGE_SIZE), F32)],
        scratch_shapes=[pltpu.VMEM((g * t, p + PAGE_SIZE), jnp.int32)],
        compiler_params=pltpu.CompilerParams(dimension_semantics=("parallel",), vmem_limit_bytes=VMEM_LIMIT),
        name="sample_select",
    )(sc, scn)


def _online_softmax_step(s, valid, v_list, m_ref, l_ref, acc_ref, values_tokens_last=False):
    m_old = m_ref[...]
    m_new = jnp.maximum(m_old, jnp.max(s, axis=1, keepdims=True))
    alpha = jnp.exp(m_old - m_new)
    p = jnp.exp(s - m_new)
    if valid is not None:
        p = jnp.where(valid, p, 0.0)
    l_ref[...] = alpha * l_ref[...] + jnp.sum(p, axis=1, keepdims=True)
    acc = alpha * acc_ref[...]
    pb = p.astype(BF16)
    for j, vj in enumerate(v_list):
        pj = pb[:, j * PAGE_SIZE:(j + 1) * PAGE_SIZE]
        if values_tokens_last:
            acc = acc + lax.dot_general(pj, vj, _NT, preferred_element_type=F32)
        else:
            acc = acc + jnp.dot(pj, vj, preferred_element_type=F32)
    acc_ref[...] = acc
    m_ref[...] = m_new


def _sattn_kernel(pt_ref, q_ref, sel_ref, seln_ref, kn_ref, vn_ref, *rest, pg, scale):
    k_refs, v_refs = rest[:pg], rest[pg:2 * pg]
    o_ref, m_ref, l_ref, acc_ref = rest[2 * pg:]
    gi = pl.program_id(1)
    rep = q_ref.shape[1] // sel_ref.shape[1]
    q = q_ref[0].astype(BF16)

    @pl.when(gi == 0)
    def _():
        m_ref[...] = jnp.full(m_ref.shape, MASK_VALUE, F32)
        l_ref[...] = jnp.zeros(l_ref.shape, F32)
        acc_ref[...] = jnp.zeros(acc_ref.shape, F32)

    def attend(k_list, v_list, sel):
        s = jnp.concatenate([jnp.dot(q, kk, preferred_element_type=F32) for kk in k_list], axis=1)
        valid = jnp.concatenate([sel] * rep, axis=0) > 0.0
        s = jnp.where(valid, s * scale, MASK_VALUE)
        _online_softmax_step(s, valid, v_list, m_ref, l_ref, acc_ref, values_tokens_last=True)

    attend([r[0, 0].astype(BF16) for r in k_refs], [r[0, 0].astype(BF16) for r in v_refs], sel_ref[0])

    @pl.when(gi == pl.num_programs(1) - 1)
    def _():
        attend([kn_ref[0].astype(BF16)], [vn_ref[0].astype(BF16)], seln_ref[0])
        o_ref[0] = acc_ref[...] / l_ref[...]


def sample_sparse_attention(q, k_new, v_new, sel, seln, pool_k, pool_v, layer, page_table, pg):
    db, t, nh, dh = q.shape
    n_kv = k_new.shape[2]
    grp = nh // n_kv
    n_pages = page_table.shape[1]
    rows = nh * t
    qh = q.transpose(0, 2, 1, 3).reshape(db, n_kv, grp * t, dh)
    q_pad = jnp.concatenate(
        [jnp.pad(qh[:, j], ((0, 0), (0, 0), (j * dh, (n_kv - 1 - j) * dh))) for j in range(n_kv)], axis=1)
    kn = _tokens_last(k_new.reshape(db, t, n_kv * dh))
    vn = _tokens_last(v_new.reshape(db, t, n_kv * dh))
    width = n_kv * dh
    grid_spec = pltpu.PrefetchScalarGridSpec(
        num_scalar_prefetch=1,
        grid=(db, n_pages // pg),
        in_specs=[pl.BlockSpec((1, rows, width), lambda b, g, pt: (b, 0, 0)),
                  pl.BlockSpec((1, t, pg * PAGE_SIZE), lambda b, g, pt: (b, 0, g)),
                  pl.BlockSpec((1, t, PAGE_SIZE), lambda b, g, pt: (b, 0, 0)),
                  pl.BlockSpec((1, width, PAGE_SIZE), lambda b, g, pt: (b, 0, 0)),
                  pl.BlockSpec((1, width, PAGE_SIZE), lambda b, g, pt: (b, 0, 0))]
                 + _page_specs(layer, n_pages, pg, width, PAGE_SIZE)
                 + _page_specs(layer, n_pages, pg, width, PAGE_SIZE),
        out_specs=pl.BlockSpec((1, rows, width), lambda b, g, pt: (b, 0, 0)),
        scratch_shapes=[pltpu.VMEM((rows, 1), F32), pltpu.VMEM((rows, 1), F32), pltpu.VMEM((rows, width), F32)],
    )
    out = pl.pallas_call(
        functools.partial(_sattn_kernel, pg=pg, scale=dh ** -0.5),
        grid_spec=grid_spec,
        out_shape=jax.ShapeDtypeStruct((db, rows, width), F32),
        compiler_params=pltpu.CompilerParams(dimension_semantics=("parallel", "arbitrary"),
                                             vmem_limit_bytes=VMEM_LIMIT),
        name="sample_sparse_attention",
    )(page_table.reshape(-1), q_pad, sel, seln, kn, vn, *([pool_k] * pg), *([pool_v] * pg))
    out = out.reshape(db, n_kv, grp, t, n_kv, dh)
    out = jnp.stack([out[:, j, :, :, j] for j in range(n_kv)], axis=1)
    return out.transpose(0, 3, 1, 2, 4).reshape(db, t, nh * dh)


def dsa_sample(q, k_new, v_new, qi, ki_new, w, past, pool_k, pool_v, pool_idx, layer, page_table):
    t = q.shape[1]
    topk = min(TOPK_MAX, (past + t) // 4)
    sc, scn = sample_indexer_scores(qi, w, ki_new, pool_idx, layer, page_table, PAGES_PER_STEP)
    sel, seln = sample_select(sc, scn, topk)
    return sample_sparse_attention(q, k_new, v_new, sel, seln, pool_k, pool_v, layer, page_table, PAGES_PER_STEP)


def _smla_kernel(pt_ref, qa_ref, qr_ref, cn_ref, rn_ref, wuv_ref, *rest, pg, scale):
    c_refs, r_refs = rest[:pg], rest[pg:2 * pg]
    o_ref, m_ref, l_ref, acc_ref = rest[2 * pg:]
    gi = pl.program_id(1)
    n_heads = wuv_ref.shape[0]
    t = qa_ref.shape[1] // n_heads
    qa = qa_ref[0].astype(BF16)
    qr = qr_ref[0].astype(BF16)

    @pl.when(gi == 0)
    def _():
        m_ref[...] = jnp.full(m_ref.shape, MASK_VALUE, F32)
        l_ref[...] = jnp.zeros(l_ref.shape, F32)
        acc_ref[...] = jnp.zeros(acc_ref.shape, F32)

    def attend(c_list, r_list, valid):
        s = jnp.concatenate(
            [lax.dot_general(qa, c, _NT, preferred_element_type=F32)
             + jnp.dot(qr, r, preferred_element_type=F32) for c, r in zip(c_list, r_list)], axis=1)
        s = s * scale
        if valid is not None:
            s = jnp.where(valid, s, MASK_VALUE)
        _online_softmax_step(s, valid, c_list, m_ref, l_ref, acc_ref)

    attend([r[0, 0].astype(BF16) for r in c_refs], [r[0, 0].astype(BF16) for r in r_refs], None)

    @pl.when(gi == pl.num_programs(1) - 1)
    def _():
        rows = qa_ref.shape[1]
        tok = lax.broadcasted_iota(jnp.int32, (rows, PAGE_SIZE), 0) & (t - 1)
        col = lax.broadcasted_iota(jnp.int32, (rows, PAGE_SIZE), 1)
        attend([cn_ref[0].astype(BF16)], [rn_ref[0].astype(BF16)], col <= tok)
        o_lat = (acc_ref[...] / l_ref[...]).astype(BF16)
        for h in range(n_heads):
            o_ref[0, h] = jnp.dot(o_lat[h * t:(h + 1) * t], wuv_ref[h].astype(BF16), preferred_element_type=F32)


def sample_mla_attention(q_abs, q_rope, c_new, r_new, w_uv, pool_ckv, pool_ckr, layer, page_table, pg, scale):
    db, t, nh, dl = q_abs.shape
    assert t & (t - 1) == 0
    dr = q_rope.shape[3]
    dv = w_uv.shape[2]
    n_pages = page_table.shape[1]
    rows = nh * t
    qa = q_abs.transpose(0, 2, 1, 3).reshape(db, rows, dl)
    qr = q_rope.transpose(0, 2, 1, 3).reshape(db, rows, dr)
    cn = jnp.pad(c_new, ((0, 0), (0, PAGE_SIZE - t), (0, 0)))
    rn = _tokens_last(r_new)
    grid_spec = pltpu.PrefetchScalarGridSpec(
        num_scalar_prefetch=1,
        grid=(db, n_pages // pg),
        in_specs=[pl.BlockSpec((1, rows, dl), lambda b, g, pt: (b, 0, 0)),
                  pl.BlockSpec((1, rows, dr), lambda b, g, pt: (b, 0, 0)),
                  pl.BlockSpec((1, PAGE_SIZE, dl), lambda b, g, pt: (b, 0, 0)),
                  pl.BlockSpec((1, dr, PAGE_SIZE), lambda b, g, pt: (b, 0, 0)),
                  pl.BlockSpec((nh, dl, dv), lambda b, g, pt: (0, 0, 0))]
                 + _page_specs(layer, n_pages, pg, PAGE_SIZE, dl) + _page_specs(layer, n_pages, pg, dr, PAGE_SIZE),
        out_specs=pl.BlockSpec((1, nh, t, dv), lambda b, g, pt: (b, 0, 0, 0)),
        scratch_shapes=[pltpu.VMEM((rows, 1), F32), pltpu.VMEM((rows, 1), F32), pltpu.VMEM((rows, dl), F32)],
    )
    out = pl.pallas_call(
        functools.partial(_smla_kernel, pg=pg, scale=scale),
        grid_spec=grid_spec,
        out_shape=jax.ShapeDtypeStruct((db, nh, t, dv), F32),
        compiler_params=pltpu.CompilerParams(dimension_semantics=("parallel", "arbitrary"),
                                             vmem_limit_bytes=VMEM_LIMIT),
        name="sample_mla_attention",
    )(page_table.reshape(-1), qa, qr, cn, rn, w_uv.transpose(1, 0, 2), *([pool_ckv] * pg), *([pool_ckr] * pg))
    return out.transpose(0, 2, 1, 3).reshape(db, t, nh * dv)


def mla_attend(q_abs, q_rope, c_kv, k_rope, q_pos, k_pos, w_uv):
    b, t = q_abs.shape[:2]
    s = jnp.einsum('bthr,bsr->bhts', q_abs, c_kv) + jnp.einsum('bthp,bsp->bhts', q_rope, k_rope)
    s = s * ((C_NOPE + C_ROPE) ** -0.5)
    s = jnp.where(k_pos[None, None, None, :] <= q_pos[None, None, :, None], s, MASK_VALUE)
    p = jax.nn.softmax(s, axis=-1)
    o_lat = jnp.einsum('bhts,bsr->bthr', p, c_kv)
    return jnp.einsum('bthr,rhv->bthv', o_lat, w_uv).reshape(b, t, C_HEADS * C_V)


def _pmla_kernel(q_ref, k_ref, wuv_ref, o_ref, m_ref, l_ref, acc_ref, *, tk, dl, scale):
    n_heads, tq, dq = q_ref.shape[1:]
    rows = n_heads * tq
    qb = pl.program_id(1)
    q = q_ref[0].reshape(rows, dq)
    m_ref[...] = jnp.full(m_ref.shape, MASK_VALUE, F32)
    l_ref[...] = jnp.zeros(l_ref.shape, F32)
    acc_ref[...] = jnp.zeros(acc_ref.shape, F32)
    qpos = qb * tq + (lax.broadcasted_iota(jnp.int32, (rows, tk), 0) & (tq - 1))
    kofs = lax.broadcasted_iota(jnp.int32, (rows, tk), 1)

    def body(c, carry):
        start = pl.multiple_of(c * tk, tk)
        kc = k_ref[0, pl.ds(start, tk), :]
        s = lax.dot_general(q, kc, _NT, preferred_element_type=F32) * scale
        s = jnp.where(start + kofs <= qpos, s, MASK_VALUE)
        m_old = m_ref[...]
        m_new = jnp.maximum(m_old, jnp.max(s, axis=1, keepdims=True))
        alpha = jnp.exp(m_old - m_new)
        p = jnp.exp(s - m_new)
        l_ref[...] = alpha * l_ref[...] + jnp.sum(p, axis=1, keepdims=True)
        acc_ref[...] = alpha * acc_ref[...] + jnp.dot(p.astype(BF16), kc[:, :dl], preferred_element_type=F32)
        m_ref[...] = m_new
        return carry

    lax.fori_loop(0, (qb * tq) // tk + 1, body, 0)
    o_lat = (acc_ref[...] / l_ref[...]).astype(BF16)
    for h in range(n_heads):
        o_ref[0, h] = jnp.dot(o_lat[h * tq:(h + 1) * tq], wuv_ref[h].astype(BF16), preferred_element_type=F32)


def mla_prompt(q_abs, q_rope, c_kv, k_rope, w_uv, tq=128, tk=512):
    b, l, nh, dl = q_abs.shape
    dr, dv = q_rope.shape[3], w_uv.shape[2]
    tk = min(tk, l)
    assert tq & (tq - 1) == 0 and l % tk == 0 and tk % tq == 0
    qcat = jnp.concatenate([q_abs, q_rope], -1).transpose(0, 2, 1, 3).astype(BF16)
    kcat = jnp.concatenate([c_kv, k_rope], -1).astype(BF16)
    out = pl.pallas_call(
        functools.partial(_pmla_kernel, tk=tk, dl=dl, scale=(C_NOPE + C_ROPE) ** -0.5),
        grid=(b, l // tq),
        in_specs=[pl.BlockSpec((1, nh, tq, dl + dr), lambda bi, qb: (bi, 0, qb, 0)),
                  pl.BlockSpec((1, l, dl + dr), lambda bi, qb: (bi, 0, 0)),
                  pl.BlockSpec((nh, dl, dv), lambda bi, qb: (0, 0, 0))],
        out_specs=pl.BlockSpec((1, nh, tq, dv), lambda bi, qb: (bi, 0, qb, 0)),
        out_shape=jax.ShapeDtypeStruct((b, nh, l, dv), F32),
        scratch_shapes=[pltpu.VMEM((nh * tq, 1), F32), pltpu.VMEM((nh * tq, 1), F32),
                        pltpu.VMEM((nh * tq, dl), F32)],
        compiler_params=pltpu.CompilerParams(dimension_semantics=("parallel", "arbitrary"),
                                             vmem_limit_bytes=VMEM_LIMIT),
        name="mla_prompt",
    )(qcat, kcat, w_uv.transpose(1, 0, 2))
    return out.transpose(0, 2, 1, 3).reshape(b, l, nh * dv)


def prompt_mixers(h, pos, w_in, lb, a_norm_g, cq_g, w_uq, ckv_g, w_uk, w_uv):
    (a_q, a_logf, a_k, a_v, a_g), (b_q, b_k, b_v, b_qi, b_ki, b_w), (c_qn, c_qr, c_kv, c_kr) = project(
        h, w_in, pos, lb, cq_g, w_uq, ckv_g)
    s0 = jnp.zeros((h.shape[0], A_HEADS, A_DK, A_DV), F32)
    o_a, s_a = hgrn2_recurrence(a_q, a_logf, a_k, a_v, s0)
    o_a = hgrn_output(o_a, a_g, a_norm_g)
    o_b = dsa_prompt(b_q, b_k, b_v, b_qi, b_ki, b_w, pos)
    q_abs = jnp.einsum('bthn,rhn->bthr', c_qn, w_uk)
    o_c = mla_prompt(q_abs, c_qr, c_kv, c_kr, w_uv)
    mix = jnp.concatenate([o_a, o_b, o_c], -1)
    return mix, (b_k, b_v, b_ki, c_kv, c_kr, s_a)


def sample_mixers(h, pos, past, layer, pool_bk, pool_bv, pool_bidx, pool_ckv, pool_ckr, s_prev, page_table,
                  w_in, lb, a_norm_g, cq_g, w_uq, ckv_g, w_uk, w_uv):
    (a_q, a_logf, a_k, a_v, a_g), (b_q, b_k, b_v, b_qi, b_ki, b_w), (c_qn, c_qr, c_kv, c_kr) = project(
        h, w_in, pos, lb, cq_g, w_uq, ckv_g)
    o_a, s_a = hgrn2_recurrence(a_q, a_logf, a_k, a_v, s_prev)
    o_a = hgrn_output(o_a, a_g, a_norm_g)
    o_b = dsa_sample(b_q, b_k, b_v, b_qi, b_ki, b_w, past, pool_bk, pool_bv, pool_bidx, layer, page_table)
    q_abs = jnp.einsum('bthn,rhn->bthr', c_qn, w_uk)
    o_c = sample_mla_attention(q_abs, c_qr, c_kv, c_kr, w_uv, pool_ckv, pool_ckr, layer, page_table,
                               PAGES_PER_STEP, (C_NOPE + C_ROPE) ** -0.5)
    mix = jnp.concatenate([o_a, o_b, o_c], -1)
    return mix, (b_k, b_v, b_ki, c_kv, c_kr, s_a)


def finish_layer(x, mix, w_o, ln1_g, ln1_b, w_gate, w_up, w_down, ln2_g, ln2_b):
    b, l, d = x.shape
    x1, x1b = matmul_res_ln(mix.reshape(b * l, d).astype(BF16), w_o, x.reshape(b * l, d), ln1_g, ln1_b)
    hmid = ffn_up(x1b, w_gate, w_up)
    x2, x2b = matmul_res_ln(hmid, w_down, x1, ln2_g, ln2_b)
    return x2.reshape(b, l, d), x2b.reshape(b, l, d)


def kernel(x_prompt, x_sample, cache_b_k, cache_b_v, cache_b_idx, cache_c_kv, cache_c_kr, state_a,
           page_table, ln_in_g, ln_in_b, w_in, hgrn_lb_logits, hgrn_norm_g, mla_q_norm_g, mla_w_uq,
           mla_kv_norm_g, mla_w_uk, mla_w_uv, w_o, ln1_g, ln1_b, w_gate, w_up, w_down, ln2_g, ln2_b):
    lower_bounds = hgrn_lower_bounds(hgrn_lb_logits)
    past = page_table.shape[1] * PAGE_SIZE
    pos_p = jnp.arange(x_prompt.shape[1], dtype=jnp.int32)
    pos_s = past + jnp.arange(x_sample.shape[1], dtype=jnp.int32)
    xp = layer_norm(x_prompt, ln_in_g, ln_in_b)
    xs = layer_norm(x_sample, ln_in_g, ln_in_b)
    kv_width = B_KV_HEADS * B_HEAD_DIM
    pool_bk = cache_b_k.transpose(0, 1, 3, 4, 2).reshape(cache_b_k.shape[:2] + (kv_width, PAGE_SIZE))
    pool_bv = cache_b_v.transpose(0, 1, 3, 4, 2).reshape(cache_b_v.shape[:2] + (kv_width, PAGE_SIZE))
    pool_bidx = cache_b_idx.transpose(0, 1, 3, 2)
    pool_ckr = cache_c_kr.transpose(0, 1, 3, 2)
    w_in_b, w_o_b, w_gate_b, w_up_b, w_down_b = (w.astype(BF16) for w in (w_in, w_o, w_gate, w_up, w_down))
    xpb, xsb = xp.astype(BF16), xs.astype(BF16)
    new_p = ([], [], [], [], [], [])
    new_s = ([], [], [], [], [], [])
    for l in range(DEPTH):
        mix_w = (w_in_b[l], lower_bounds[l], hgrn_norm_g[l], mla_q_norm_g[l], mla_w_uq[l],
                 mla_kv_norm_g[l], mla_w_uk[l], mla_w_uv[l])
        ffn_w = (w_o_b[l], ln1_g[l], ln1_b[l], w_gate_b[l], w_up_b[l], w_down_b[l], ln2_g[l], ln2_b[l])
        mix_p, st_p = prompt_mixers(xpb, pos_p, *mix_w)
        mix_s, st_s = sample_mixers(xsb, pos_s, past, l, pool_bk, pool_bv, pool_bidx,
                                    cache_c_kv, pool_ckr, state_a[l], page_table, *mix_w)
        xp, xpb = finish_layer(xp, mix_p, *ffn_w)
        xs, xsb = finish_layer(xs, mix_s, *ffn_w)
        for lst, st in zip(new_p, st_p):
            lst.append(st)
        for lst, st in zip(new_s, st_s):
            lst.append(st)
    b_k_p, b_v_p, b_idx_p, c_kv_p, c_kr_p, s_a_p = [jnp.stack(v, 0) for v in new_p]
    b_k_s, b_v_s, b_idx_s, c_kv_s, c_kr_s, s_a_s = [jnp.stack(v, 0) for v in new_s]
    return (xp, xs, b_k_p, b_v_p, b_idx_p, c_kv_p, c_kr_p, s_a_p,
            b_k_s, b_v_s, b_idx_s, c_kv_s, c_kr_s, s_a_s)
```

```python
import functools

import jax
import jax.numpy as jnp
import numpy as np
from jax import lax
from jax.experimental import pallas as pl
from jax.experimental.pallas import tpu as pltpu

D_MODEL = 1024
DEPTH = 4
PAGE_SIZE = 128
A_HEADS = 4
A_DK = 128
A_DV = 128
B_HEADS = 4
B_KV_HEADS = 2
B_HEAD_DIM = 64
IDX_HEADS = 8
IDX_DIM = 64
TOPK_MAX = 256
C_HEADS = 4
C_NOPE = 64
C_ROPE = 32
C_V = 64
C_Q_LORA = 256
C_KV_LORA = 128
ROPE_THETA = 10000.0
EPS = 1e-5
MASK_VALUE = -1e30
DEEPNORM_ALPHA = (2 * DEPTH) ** 0.25
IN_SPLITS = (A_HEADS * A_DK, A_HEADS * A_DK, A_HEADS * A_DV, A_HEADS * A_DV,
             B_HEADS * B_HEAD_DIM, B_KV_HEADS * B_HEAD_DIM, B_KV_HEADS * B_HEAD_DIM,
             IDX_HEADS * IDX_DIM, IDX_DIM, IDX_HEADS,
             C_Q_LORA, C_KV_LORA, C_ROPE)
F32 = jnp.float32
BF16 = jnp.bfloat16


def _mm_kernel(x_ref, w_ref, o_ref):
    o_ref[...] = jnp.dot(x_ref[...], w_ref[...], preferred_element_type=F32)


def _matmul(x, w, tm=1024, tn=512):
    m, k = x.shape
    n = w.shape[1]
    tm = min(tm, m)
    tn = min(tn, n)
    return pl.pallas_call(
        _mm_kernel,
        grid=(pl.cdiv(m, tm), pl.cdiv(n, tn)),
        in_specs=[pl.BlockSpec((tm, k), lambda i, j: (i, 0)),
                  pl.BlockSpec((k, tn), lambda i, j: (0, j))],
        out_specs=pl.BlockSpec((tm, tn), lambda i, j: (i, j)),
        out_shape=jax.ShapeDtypeStruct((m, n), F32),
        compiler_params=pltpu.CompilerParams(dimension_semantics=("parallel", "parallel"),
                                             vmem_limit_bytes=48 * 1024 * 1024),
        name="dense_matmul",
    )(x, w)


def _mm3(h, w):
    b, l, d = h.shape
    return _matmul(h.reshape(b * l, d), w).reshape(b, l, w.shape[1])


def _ffn_up_kernel(x_ref, wg_ref, wu_ref, h_ref):
    x = x_ref[...]
    g = jnp.dot(x, wg_ref[...], preferred_element_type=F32)
    u = jnp.dot(x, wu_ref[...], preferred_element_type=F32)
    h_ref[...] = (g * jax.nn.sigmoid(g) * u).astype(BF16)


def ffn_up(x, w_gate, w_up, tm=512):
    m, d = x.shape
    f = w_gate.shape[1]
    tm = min(tm, m)
    tn = f // 2
    assert m % tm == 0 and tn % 128 == 0
    return pl.pallas_call(
        _ffn_up_kernel,
        grid=(m // tm, 2),
        in_specs=[pl.BlockSpec((tm, d), lambda i, j: (i, 0)),
                  pl.BlockSpec((d, tn), lambda i, j: (0, j)),
                  pl.BlockSpec((d, tn), lambda i, j: (0, j))],
        out_specs=pl.BlockSpec((tm, tn), lambda i, j: (i, j)),
        out_shape=jax.ShapeDtypeStruct((m, f), BF16),
        compiler_params=pltpu.CompilerParams(dimension_semantics=("parallel", "parallel"),
                                             vmem_limit_bytes=48 * 1024 * 1024),
        name="ffn_up",
    )(x, w_gate, w_up)


def _mm_res_ln_kernel(a_ref, w_ref, res_ref, g_ref, b_ref, o_ref, ob_ref):
    y = DEEPNORM_ALPHA * res_ref[...] + jnp.dot(a_ref[...], w_ref[...], preferred_element_type=F32)
    mu = jnp.mean(y, axis=-1, keepdims=True)
    var = jnp.mean(jnp.square(y - mu), axis=-1, keepdims=True)
    out = (y - mu) * lax.rsqrt(var + EPS) * g_ref[...] + b_ref[...]
    o_ref[...] = out
    ob_ref[...] = out.astype(BF16)


def matmul_res_ln(a, w, res, g, b, tm=512):
    m, k = a.shape
    d = w.shape[1]
    tm = min(tm, m)
    assert m % tm == 0
    return pl.pallas_call(
        _mm_res_ln_kernel,
        grid=(m // tm,),
        in_specs=[pl.BlockSpec((tm, k), lambda i: (i, 0)),
                  pl.BlockSpec((k, d), lambda i: (0, 0)),
                  pl.BlockSpec((tm, d), lambda i: (i, 0)),
                  pl.BlockSpec((1, d), lambda i: (0, 0)),
                  pl.BlockSpec((1, d), lambda i: (0, 0))],
        out_specs=[pl.BlockSpec((tm, d), lambda i: (i, 0)), pl.BlockSpec((tm, d), lambda i: (i, 0))],
        out_shape=[jax.ShapeDtypeStruct((m, d), F32), jax.ShapeDtypeStruct((m, d), BF16)],
        compiler_params=pltpu.CompilerParams(dimension_semantics=("parallel",),
                                             vmem_limit_bytes=48 * 1024 * 1024),
        name="matmul_res_ln",
    )(a, w, res, g.reshape(1, d), b.reshape(1, d))


def layer_norm(x, g, b):
    mu = jnp.mean(x, -1, keepdims=True)
    var = jnp.mean(jnp.square(x - mu), -1, keepdims=True)
    return (x - mu) * lax.rsqrt(var + EPS) * g + b


def rms_norm(x, g):
    return x * lax.rsqrt(jnp.mean(jnp.square(x), -1, keepdims=True) + EPS) * g


def rope(x, pos):
    half = x.shape[-1] // 2
    inv = ROPE_THETA ** (-jnp.arange(half, dtype=F32) / half)
    ang = pos.astype(F32)[:, None] * inv[None, :]
    shape = (ang.shape[0],) + (1,) * (x.ndim - 3) + (half,)
    cos = jnp.cos(ang).reshape(shape)
    sin = jnp.sin(ang).reshape(shape)
    x1, x2 = x[..., :half], x[..., half:]
    return jnp.concatenate([x1 * cos - x2 * sin, x2 * cos + x1 * sin], -1)


def hgrn_lower_bounds(logits):
    p = jax.nn.softmax(logits.astype(F32), axis=0)
    return jnp.clip(jnp.cumsum(p, axis=0) - p[0:1], 0.0, 1.0 - 1e-4)


def project(y, pos, cq_g, w_uq, ckv_g):
    b, l = y.shape[:2]
    offs = [int(o) for o in np.cumsum(IN_SPLITS)[:-1]]
    _, _, _, _, bq, bk, bv, bqi, bki, bw, cdq, cdkv, ckr = jnp.split(y, offs, axis=-1)
    b_q = rope(bq.reshape(b, l, B_HEADS, B_HEAD_DIM), pos)
    b_k = rope(bk.reshape(b, l, B_KV_HEADS, B_HEAD_DIM), pos)
    b_v = bv.reshape(b, l, B_KV_HEADS, B_HEAD_DIM)
    b_qi = rope(bqi.reshape(b, l, IDX_HEADS, IDX_DIM), pos)
    b_ki = rope(bki[:, :, None, :], pos)[:, :, 0]
    b_w = bw * (IDX_HEADS ** -0.5 * IDX_DIM ** -0.5)
    cq = jnp.einsum('blr,rhd->blhd', rms_norm(cdq, cq_g), w_uq)
    c_qn = cq[..., :C_NOPE]
    c_qr = rope(cq[..., C_NOPE:], pos)
    c_kv = rms_norm(cdkv, ckv_g)
    c_kr = rope(ckr[:, :, None, :], pos)[:, :, 0]
    return (b_q, b_k, b_v, b_qi, b_ki, b_w), (c_qn, c_qr, c_kv, c_kr)


_TN =(((0,), (0,)), ((), ()))
HGRN_CHUNK = 64
HGRN_SUB = 8


def _split3(x):
    hi = x.astype(BF16)
    r = x - hi.astype(F32)
    mid = r.astype(BF16)
    return hi, mid, (r - mid.astype(F32)).astype(BF16)


def _hgrn_kernel(q_ref, z_ref, v_ref, g_ref, lb_ref, ng_ref, s0_ref, o_ref, sfin_ref, st_ref):
    c = pl.program_id(1)

    @pl.when(c == 0)
    def _():
        st_ref[...] = s0_ref[0]

    n = q_ref.shape[1]
    n_heads, _, dk = lb_ref.shape
    sb = min(HGRN_SUB, n)
    tri = lax.broadcasted_iota(jnp.int32, (n, n), 0) >= lax.broadcasted_iota(jnp.int32, (n, n), 1)
    tri = jnp.where(tri, 1.0, 0.0).astype(BF16)
    srow = lax.broadcasted_iota(jnp.int32, (sb, dk), 0)
    for h in range(n_heads):
        cols = slice(h * dk, (h + 1) * dk)
        q, v = q_ref[0, :, cols], v_ref[0, :, cols]
        lb = lb_ref[h]
        f = lb + (1.0 - lb) * jax.nn.sigmoid(z_ref[0, :, cols])
        logf = jnp.log(jnp.maximum(f, 1e-30))
        k = 1.0 - f
        cum = sum(jnp.dot(tri, piece, preferred_element_type=F32) for piece in _split3(logf))
        st = st_ref[h]
        o_inter = lax.dot_general((q * jnp.exp(cum)).astype(BF16), st.astype(BF16), _NT,
                                  preferred_element_type=F32)
        blocks = []
        for i in range(n // sb):
            lo = i * sb
            cum_i, q_i, k_i, v_i = cum[lo:lo + sb], q[lo:lo + sb], k[lo:lo + sb], v[lo:lo + sb]
            o_i = jnp.zeros((sb, dk), F32)
            if i > 0:
                start = cum[lo - 1:lo]
                qs = (q_i * jnp.exp(cum_i - start)).astype(BF16)
                ks = (k[:lo] * jnp.exp(start - cum[:lo])).astype(BF16)
                a = lax.dot_general(qs, ks, _NT, preferred_element_type=F32)
                o_i = jnp.dot(a.astype(BF16), v[:lo].astype(BF16), preferred_element_type=F32)
            for t in range(sb):
                decay = jnp.exp(jnp.where(srow <= t, cum_i[t:t + 1] - cum_i, MASK_VALUE))
                a_t = jnp.sum(q_i[t:t + 1] * k_i * decay, axis=1, keepdims=True)
                row = jnp.sum(a_t * v_i, axis=0, keepdims=True)
                o_i = o_i + jnp.where(srow == t, row, 0.0)
            blocks.append(o_i)
        o = o_inter + (jnp.concatenate(blocks, axis=0) if len(blocks) > 1 else blocks[0])
        last = cum[n - 1:n]
        kd = (k * jnp.exp(last - cum)).astype(BF16)
        st_new = jnp.exp(last) * st + lax.dot_general(v.astype(BF16), kd, _TN, preferred_element_type=F32)
        st_ref[h] = st_new
        sfin_ref[0, h] = st_new
        o = o * lax.rsqrt(jnp.mean(jnp.square(o), axis=-1, keepdims=True) + EPS) * ng_ref[h]
        g = g_ref[0, :, cols]
        o_ref[0, :, cols] = o * (g * jax.nn.sigmoid(g))


def hgrn2_mixer(y, lb, norm_g, s0, n_heads, dk):
    b, l, _ = y.shape
    c = min(HGRN_CHUNK, l)
    width = n_heads * dk
    assert l % c == 0

    def col(j):
        return pl.BlockSpec((1, c, width), lambda bi, ci: (bi, ci, j))

    vec = pl.BlockSpec((n_heads, 1, dk), lambda bi, ci: (0, 0, 0))
    state = pl.BlockSpec((1, n_heads, dk, dk), lambda bi, ci: (bi, 0, 0, 0))
    o, s_t = pl.pallas_call(
        _hgrn_kernel,
        grid=(b, l // c),
        in_specs=[col(0), col(1), col(2), col(3), vec, vec, state],
        out_specs=[pl.BlockSpec((1, c, width), lambda bi, ci: (bi, ci, 0)), state],
        out_shape=[jax.ShapeDtypeStruct((b, l, width), F32),
                   jax.ShapeDtypeStruct((b, n_heads, dk, dk), F32)],
        scratch_shapes=[pltpu.VMEM((n_heads, dk, dk), F32)],
        compiler_params=pltpu.CompilerParams(dimension_semantics=("parallel", "arbitrary"),
                                             vmem_limit_bytes=48 * 1024 * 1024),
        name="hgrn2_mixer",
    )(y, y, y, y, lb.reshape(n_heads, 1, dk), norm_g.reshape(n_heads, 1, dk), s0.transpose(0, 1, 3, 2))
    return o, s_t.transpose(0, 1, 3, 2)


INT_MIN = -2 ** 31
_NT = (((1,), (1,)), ((), ()))


REDUCE_LANES = 64
KEY_STEP = 512


def _reduce_rows(x, op):
    rows = x.shape[0]
    if rows > REDUCE_LANES and rows % REDUCE_LANES == 0:
        x = op(x.reshape(rows // REDUCE_LANES, REDUCE_LANES, x.shape[1]), axis=0)
    return op(x, axis=0, keepdims=True)


def _dsa_prompt_body(lk, qi_ref, wT_ref, ki_ref, q_ref, k_ref, vT_ref, o_ref, key_ref, sel_ref, *, topk, scale):
    n_idx, tq = qi_ref.shape[1], qi_ref.shape[2]
    qb = pl.program_id(1)
    ki = ki_ref[0, :lk].astype(BF16)
    acc = jnp.zeros((lk, tq), F32)
    for h in range(n_idx):
        d = lax.dot_general(ki, qi_ref[0, h].astype(BF16), _NT, preferred_element_type=F32)
        acc = acc + wT_ref[0, h:h + 1, :] * jnp.maximum(d, 0.0)
    kpos = lax.broadcasted_iota(jnp.int32, (lk, tq), 0)
    qpos = qb * tq + lax.broadcasted_iota(jnp.int32, (lk, tq), 1)
    causal = kpos <= qpos
    s = jnp.where(causal, acc, MASK_VALUE)
    s = jnp.where(s == 0.0, 0.0, s)
    bits = pltpu.bitcast(s, jnp.int32)
    key_ref[:lk] = bits ^ ((bits >> 31) & 0x7FFFFFFF)

    def count_ge(cand):
        return _reduce_rows(jnp.where(key_ref[:lk] >= cand, 1.0, 0.0), jnp.sum)

    kf = float(topk)
    t0 = jnp.where(count_ge(jnp.zeros((1, tq), jnp.int32)) >= kf, 0, INT_MIN).astype(jnp.int32)

    def step(i, t):
        cand = t | jnp.left_shift(jnp.int32(1), 30 - i)
        return jnp.where(count_ge(cand) >= kf, cand, t)

    thr = lax.fori_loop(0, 31, step, t0)
    need = kf - _reduce_rows(jnp.where(key_ref[:lk] > thr, 1.0, 0.0), jnp.sum)
    tri = (lax.broadcasted_iota(jnp.int32, (128, 128), 0) >= lax.broadcasted_iota(jnp.int32, (128, 128), 1))
    tri = jnp.where(tri, 1.0, 0.0).astype(BF16)
    offset = jnp.zeros((1, tq), F32)
    for c in range(lk // 128):
        key_c = key_ref[c * 128:(c + 1) * 128, :]
        tie = key_c == thr
        pref = jnp.dot(tri, jnp.where(tie, 1.0, 0.0).astype(BF16), preferred_element_type=F32) + offset
        offset = pref[127:128, :]
        chosen = jnp.where(key_c > thr, 1.0, jnp.where(tie, jnp.where(pref <= need, 1.0, 0.0), 0.0))
        sel_ref[c * 128:(c + 1) * 128, :] = jnp.where(causal[c * 128:(c + 1) * 128, :], chosen, 0.0)

    n_kv = k_ref.shape[1]
    grp = q_ref.shape[1] // n_kv
    for j in range(n_kv):
        kj = k_ref[0, j, :lk].astype(BF16)
        vTj = vT_ref[0, j, :, :lk].astype(BF16)
        for g in range(grp):
            h = j * grp + g
            sT = lax.dot_general(kj, q_ref[0, h].astype(BF16), _NT, preferred_element_type=F32) * scale
            sT = jnp.where(sel_ref[:lk] > 0.0, sT, MASK_VALUE)
            m = _reduce_rows(sT, jnp.max)
            p = jnp.exp(sT - m)
            l = _reduce_rows(p, jnp.sum)
            oT = jnp.dot(vTj, p.astype(BF16), preferred_element_type=F32)
            o_ref[0, h] = oT / l


def _dsa_prompt_kernel(qi_ref, wT_ref, ki_ref, *rest, topk, scale):
    tq, L = qi_ref.shape[2], ki_ref.shape[1]
    step = min(KEY_STEP, L)
    assert L % step == 0 and step % tq == 0 and step >= topk
    extent = (pl.program_id(1) * tq) // step
    for i in range(L // step):
        pl.when(extent == i)(functools.partial(_dsa_prompt_body, (i + 1) * step, qi_ref, wT_ref, ki_ref, *rest,
                                               topk=topk, scale=scale))


def dsa_prompt(q, k, v, qi, ki, w, pos, tq=128):
    b, l, nh, dh = q.shape
    n_idx, n_kv = qi.shape[2], k.shape[2]
    topk = min(TOPK_MAX, l // 4)
    out = pl.pallas_call(
        functools.partial(_dsa_prompt_kernel, topk=topk, scale=dh ** -0.5),
        grid=(b, l // tq),
        in_specs=[
            pl.BlockSpec((1, n_idx, tq, qi.shape[3]), lambda bi, qb: (bi, 0, qb, 0)),
            pl.BlockSpec((1, n_idx, tq), lambda bi, qb: (bi, 0, qb)),
            pl.BlockSpec((1, l, ki.shape[2]), lambda bi, qb: (bi, 0, 0)),
            pl.BlockSpec((1, nh, tq, dh), lambda bi, qb: (bi, 0, qb, 0)),
            pl.BlockSpec((1, n_kv, l, dh), lambda bi, qb: (bi, 0, 0, 0)),
            pl.BlockSpec((1, n_kv, dh, l), lambda bi, qb: (bi, 0, 0, 0)),
        ],
        out_specs=pl.BlockSpec((1, nh, dh, tq), lambda bi, qb: (bi, 0, 0, qb)),
        out_shape=jax.ShapeDtypeStruct((b, nh, dh, l), F32),
        scratch_shapes=[pltpu.VMEM((l, tq), jnp.int32), pltpu.VMEM((l, tq), F32)],
        compiler_params=pltpu.CompilerParams(dimension_semantics=("parallel", "arbitrary"),
                                             vmem_limit_bytes=48 * 1024 * 1024),
        name="dsa_prompt",
    )(qi.transpose(0, 2, 1, 3), w.transpose(0, 2, 1), ki, q.transpose(0, 2, 1, 3),
      k.transpose(0, 2, 1, 3), v.transpose(0, 2, 3, 1))
    return out.transpose(0, 3, 1, 2).reshape(b, l, nh * dh)


VMEM_LIMIT = 48 * 1024 * 1024
PAGES_PER_STEP = 32


def _page_specs(layer, n_pages, pg, rows, cols):
    def spec(j):
        return pl.BlockSpec((1, 1, rows, cols),
                            lambda b, g, pt: (layer, pt[b * n_pages + g * pg + j], 0, 0))
    return [spec(j) for j in range(pg)]


def _tokens_last(x_new):
    t = x_new.shape[1]
    return jnp.pad(x_new.transpose(0, 2, 1), ((0, 0), (0, 0), (0, PAGE_SIZE - t)))


def _sidx_kernel(pt_ref, qi_ref, w_ref, kn_ref, *rest, pg):
    page_refs, (sc_ref, scn_ref) = rest[:pg], rest[pg:]
    n_tok = sc_ref.shape[1]
    qi = qi_ref[0].astype(BF16)
    w = w_ref[0]

    def score(keys_t):
        d = jnp.dot(qi, keys_t.astype(BF16), preferred_element_type=F32)
        r = w * jnp.maximum(d, 0.0)
        acc = jnp.zeros((n_tok, PAGE_SIZE), F32)
        for h in range(r.shape[0] // n_tok):
            acc = acc + r[h * n_tok:(h + 1) * n_tok]
        return acc

    for j in range(pg):
        sc_ref[0, :, j * PAGE_SIZE:(j + 1) * PAGE_SIZE] = score(page_refs[j][0, 0])
    row = lax.broadcasted_iota(jnp.int32, (n_tok, PAGE_SIZE), 0)
    col = lax.broadcasted_iota(jnp.int32, (n_tok, PAGE_SIZE), 1)
    scn_ref[0] = jnp.where(col <= row, score(kn_ref[0]), MASK_VALUE)


def sample_indexer_scores(qi, w, ki_new, pool_idx, layer, page_table, pg):
    db, t, nh, d = qi.shape
    n_pages = page_table.shape[1]
    qi_s = qi.transpose(0, 2, 1, 3).reshape(db, nh * t, d)
    w_s = w.transpose(0, 2, 1).reshape(db, nh * t, 1)
    kn = _tokens_last(ki_new)
    grid_spec = pltpu.PrefetchScalarGridSpec(
        num_scalar_prefetch=1,
        grid=(db, n_pages // pg),
        in_specs=[pl.BlockSpec((1, nh * t, d), lambda b, g, pt: (b, 0, 0)),
                  pl.BlockSpec((1, nh * t, 1), lambda b, g, pt: (b, 0, 0)),
                  pl.BlockSpec((1, d, PAGE_SIZE), lambda b, g, pt: (b, 0, 0))]
                 + _page_specs(layer, n_pages, pg, d, PAGE_SIZE),
        out_specs=[pl.BlockSpec((1, t, pg * PAGE_SIZE), lambda b, g, pt: (b, 0, g)),
                   pl.BlockSpec((1, t, PAGE_SIZE), lambda b, g, pt: (b, 0, 0))],
    )
    return pl.pallas_call(
        functools.partial(_sidx_kernel, pg=pg),
        grid_spec=grid_spec,
        out_shape=[jax.ShapeDtypeStruct((db, t, n_pages * PAGE_SIZE), F32),
                   jax.ShapeDtypeStruct((db, t, PAGE_SIZE), F32)],
        compiler_params=pltpu.CompilerParams(dimension_semantics=("parallel", "arbitrary"),
                                             vmem_limit_bytes=VMEM_LIMIT),
        name="sample_indexer",
    )(page_table.reshape(-1), qi_s, w_s, kn, *([pool_idx] * pg))


def _ssel_kernel(sc_ref, scn_ref, sel_ref, seln_ref, key_ref, *, topk):
    g, t, p = sc_ref.shape
    rows = g * t
    n_chunks = p // PAGE_SIZE + 1

    def to_key(s):
        s = jnp.where(s == 0.0, 0.0, s)
        bits = pltpu.bitcast(s, jnp.int32)
        return bits ^ ((bits >> 31) & 0x7FFFFFFF)

    key_ref[:, :p] = to_key(sc_ref[...].reshape(rows, p))
    key_ref[:, p:] = to_key(scn_ref[...].reshape(rows, PAGE_SIZE))

    def count_ge(cand):
        return jnp.sum(jnp.where(key_ref[...] >= cand, 1.0, 0.0), axis=1, keepdims=True)

    kf = float(topk)
    t0 = jnp.where(count_ge(jnp.zeros((rows, 1), jnp.int32)) >= kf, 0, INT_MIN).astype(jnp.int32)

    def step(i, thr):
        cand = thr | jnp.left_shift(jnp.int32(1), 30 - i)
        return jnp.where(count_ge(cand) >= kf, cand, thr)

    thr = lax.fori_loop(0, 31, step, t0)
    need = kf - jnp.sum(jnp.where(key_ref[...] > thr, 1.0, 0.0), axis=1, keepdims=True)
    tri = (lax.broadcasted_iota(jnp.int32, (PAGE_SIZE, PAGE_SIZE), 0)
           <= lax.broadcasted_iota(jnp.int32, (PAGE_SIZE, PAGE_SIZE), 1))
    tri = jnp.where(tri, 1.0, 0.0).astype(BF16)
    offset = jnp.zeros((rows, 1), F32)
    trow = lax.broadcasted_iota(jnp.int32, (rows, PAGE_SIZE), 0) & (t - 1)
    tcol = lax.broadcasted_iota(jnp.int32, (rows, PAGE_SIZE), 1)
    for c in range(n_chunks):
        key_c = key_ref[:, c * PAGE_SIZE:(c + 1) * PAGE_SIZE]
        tie = key_c == thr
        pref = jnp.dot(jnp.where(tie, 1.0, 0.0).astype(BF16), tri, preferred_element_type=F32) + offset
        offset = pref[:, PAGE_SIZE - 1:PAGE_SIZE]
        chosen = jnp.where(key_c > thr, 1.0, jnp.where(tie, jnp.where(pref <= need, 1.0, 0.0), 0.0))
        if c < n_chunks - 1:
            sel_ref[:, :, c * PAGE_SIZE:(c + 1) * PAGE_SIZE] = chosen.reshape(g, t, PAGE_SIZE)
        else:
            seln_ref[...] = jnp.where(tcol <= trow, chosen, 0.0).reshape(g, t, PAGE_SIZE)


def sample_select(sc, scn, topk, g=8):
    db, t, p = sc.shape
    assert t & (t - 1) == 0 and db % g == 0
    return pl.pallas_call(
        functools.partial(_ssel_kernel, topk=topk),
        grid=(db // g,),
        in_specs=[pl.BlockSpec((g, t, p), lambda i: (i, 0, 0)),
                  pl.BlockSpec((g, t, PAGE_SIZE), lambda i: (i, 0, 0))],
        out_specs=[pl.BlockSpec((g, t, p), lambda i: (i, 0, 0)),
                   pl.BlockSpec((g, t, PAGE_SIZE), lambda i: (i, 0, 0))],
        out_shape=[jax.ShapeDtypeStruct((db, t, p), F32), jax.ShapeDtypeStruct((db, t, PAGE_SIZE), F32)],
        scratch_shapes=[pltpu.VMEM((g * t, p + PAGE_SIZE), jnp.int32)],
        compiler_params=pltpu.CompilerParams(dimension_semantics=("parallel",), vmem_limit_bytes=VMEM_LIMIT),
        name="sample_select",
    )(sc, scn)


def _online_softmax_step(s, valid, v_list, m_ref, l_ref, acc_ref, values_tokens_last=False):
    m_old = m_ref[...]
    m_new = jnp.maximum(m_old, jnp.max(s, axis=1, keepdims=True))
    alpha = jnp.exp(m_old - m_new)
    p = jnp.exp(s - m_new)
    if valid is not None:
        p = jnp.where(valid, p, 0.0)
    l_ref[...] = alpha * l_ref[...] + jnp.sum(p, axis=1, keepdims=True)
    acc = alpha * acc_ref[...]
    pb = p.astype(BF16)
    for j, vj in enumerate(v_list):
        pj = pb[:, j * PAGE_SIZE:(j + 1) * PAGE_SIZE]
        if values_tokens_last:
            acc = acc + lax.dot_general(pj, vj, _NT, preferred_element_type=F32)
        else:
            acc = acc + jnp.dot(pj, vj, preferred_element_type=F32)
    acc_ref[...] = acc
    m_ref[...] = m_new


def _sattn_kernel(pt_ref, q_ref, sel_ref, seln_ref, kn_ref, vn_ref, *rest, pg, scale):
    k_refs, v_refs = rest[:pg], rest[pg:2 * pg]
    o_ref, m_ref, l_ref, acc_ref = rest[2 * pg:]
    gi = pl.program_id(1)
    rep = q_ref.shape[1] // sel_ref.shape[1]
    q = q_ref[0].astype(BF16)

    @pl.when(gi == 0)
    def _():
        m_ref[...] = jnp.full(m_ref.shape, MASK_VALUE, F32)
        l_ref[...] = jnp.zeros(l_ref.shape, F32)
        acc_ref[...] = jnp.zeros(acc_ref.shape, F32)

    def attend(k_list, v_list, sel):
        s = jnp.concatenate([jnp.dot(q, kk, preferred_element_type=F32) for kk in k_list], axis=1)
        valid = jnp.concatenate([sel] * rep, axis=0) > 0.0
        s = jnp.where(valid, s * scale, MASK_VALUE)
        _online_softmax_step(s, valid, v_list, m_ref, l_ref, acc_ref, values_tokens_last=True)

    attend([r[0, 0].astype(BF16) for r in k_refs], [r[0, 0].astype(BF16) for r in v_refs], sel_ref[0])

    @pl.when(gi == pl.num_programs(1) - 1)
    def _():
        attend([kn_ref[0].astype(BF16)], [vn_ref[0].astype(BF16)], seln_ref[0])
        o_ref[0] = acc_ref[...] / l_ref[...]


def sample_sparse_attention(q, k_new, v_new, sel, seln, pool_k, pool_v, layer, page_table, pg):
    db, t, nh, dh = q.shape
    n_kv = k_new.shape[2]
    grp = nh // n_kv
    n_pages = page_table.shape[1]
    rows = nh * t
    qh = q.transpose(0, 2, 1, 3).reshape(db, n_kv, grp * t, dh)
    q_pad = jnp.concatenate(
        [jnp.pad(qh[:, j], ((0, 0), (0, 0), (j * dh, (n_kv - 1 - j) * dh))) for j in range(n_kv)], axis=1)
    kn = _tokens_last(k_new.reshape(db, t, n_kv * dh))
    vn = _tokens_last(v_new.reshape(db, t, n_kv * dh))
    width = n_kv * dh
    grid_spec = pltpu.PrefetchScalarGridSpec(
        num_scalar_prefetch=1,
        grid=(db, n_pages // pg),
        in_specs=[pl.BlockSpec((1, rows, width), lambda b, g, pt: (b, 0, 0)),
                  pl.BlockSpec((1, t, pg * PAGE_SIZE), lambda b, g, pt: (b, 0, g)),
                  pl.BlockSpec((1, t, PAGE_SIZE), lambda b, g, pt: (b, 0, 0)),
                  pl.BlockSpec((1, width, PAGE_SIZE), lambda b, g, pt: (b, 0, 0)),
                  pl.BlockSpec((1, width, PAGE_SIZE), lambda b, g, pt: (b, 0, 0))]
                 + _page_specs(layer, n_pages, pg, width, PAGE_SIZE)
                 + _page_specs(layer, n_pages, pg, width, PAGE_SIZE),
        out_specs=pl.BlockSpec((1, rows, width), lambda b, g, pt: (b, 0, 0)),
        scratch_shapes=[pltpu.VMEM((rows, 1), F32), pltpu.VMEM((rows, 1), F32), pltpu.VMEM((rows, width), F32)],
    )
    out = pl.pallas_call(
        functools.partial(_sattn_kernel, pg=pg, scale=dh ** -0.5),
        grid_spec=grid_spec,
        out_shape=jax.ShapeDtypeStruct((db, rows, width), F32),
        compiler_params=pltpu.CompilerParams(dimension_semantics=("parallel", "arbitrary"),
                                             vmem_limit_bytes=VMEM_LIMIT),
        name="sample_sparse_attention",
    )(page_table.reshape(-1), q_pad, sel, seln, kn, vn, *([pool_k] * pg), *([pool_v] * pg))
    out = out.reshape(db, n_kv, grp, t, n_kv, dh)
    out = jnp.stack([out[:, j, :, :, j] for j in range(n_kv)], axis=1)
    return out.transpose(0, 3, 1, 2, 4).reshape(db, t, nh * dh)


def dsa_sample(q, k_new, v_new, qi, ki_new, w, past, pool_k, pool_v, pool_idx, layer, page_table):
    t = q.shape[1]
    topk = min(TOPK_MAX, (past + t) // 4)
    sc, scn = sample_indexer_scores(qi, w, ki_new, pool_idx, layer, page_table, PAGES_PER_STEP)
    sel, seln = sample_select(sc, scn, topk)
    return sample_sparse_attention(q, k_new, v_new, sel, seln, pool_k, pool_v, layer, page_table, PAGES_PER_STEP)


def _smla_kernel(pt_ref, qa_ref, qr_ref, cn_ref, rn_ref, wuv_ref, *rest, pg, scale):
    c_refs, r_refs = rest[:pg], rest[pg:2 * pg]
    o_ref, m_ref, l_ref, acc_ref = rest[2 * pg:]
    gi = pl.program_id(1)
    n_heads = wuv_ref.shape[0]
    t = qa_ref.shape[1] // n_heads
    qa = qa_ref[0].astype(BF16)
    qr = qr_ref[0].astype(BF16)

    @pl.when(gi == 0)
    def _():
        m_ref[...] = jnp.full(m_ref.shape, MASK_VALUE, F32)
        l_ref[...] = jnp.zeros(l_ref.shape, F32)
        acc_ref[...] = jnp.zeros(acc_ref.shape, F32)

    def attend(c_list, r_list, valid):
        s = jnp.concatenate(
            [lax.dot_general(qa, c, _NT, preferred_element_type=F32)
             + jnp.dot(qr, r, preferred_element_type=F32) for c, r in zip(c_list, r_list)], axis=1)
        s = s * scale
        if valid is not None:
            s = jnp.where(valid, s, MASK_VALUE)
        _online_softmax_step(s, valid, c_list, m_ref, l_ref, acc_ref)

    attend([r[0, 0].astype(BF16) for r in c_refs], [r[0, 0].astype(BF16) for r in r_refs], None)

    @pl.when(gi == pl.num_programs(1) - 1)
    def _():
        rows = qa_ref.shape[1]
        tok = lax.broadcasted_iota(jnp.int32, (rows, PAGE_SIZE), 0) & (t - 1)
        col = lax.broadcasted_iota(jnp.int32, (rows, PAGE_SIZE), 1)
        attend([cn_ref[0].astype(BF16)], [rn_ref[0].astype(BF16)], col <= tok)
        o_lat = (acc_ref[...] / l_ref[...]).astype(BF16)
        for h in range(n_heads):
            o_ref[0, h] = jnp.dot(o_lat[h * t:(h + 1) * t], wuv_ref[h].astype(BF16), preferred_element_type=F32)


def sample_mla_attention(q_abs, q_rope, c_new, r_new, w_uv, pool_ckv, pool_ckr, layer, page_table, pg, scale):
    db, t, nh, dl = q_abs.shape
    assert t & (t - 1) == 0
    dr = q_rope.shape[3]
    dv = w_uv.shape[2]
    n_pages = page_table.shape[1]
    rows = nh * t
    qa = q_abs.transpose(0, 2, 1, 3).reshape(db, rows, dl)
    qr = q_rope.transpose(0, 2, 1, 3).reshape(db, rows, dr)
    cn = jnp.pad(c_new, ((0, 0), (0, PAGE_SIZE - t), (0, 0)))
    rn = _tokens_last(r_new)
    grid_spec = pltpu.PrefetchScalarGridSpec(
        num_scalar_prefetch=1,
        grid=(db, n_pages // pg),
        in_specs=[pl.BlockSpec((1, rows, dl), lambda b, g, pt: (b, 0, 0)),
                  pl.BlockSpec((1, rows, dr), lambda b, g, pt: (b, 0, 0)),
                  pl.BlockSpec((1, PAGE_SIZE, dl), lambda b, g, pt: (b, 0, 0)),
                  pl.BlockSpec((1, dr, PAGE_SIZE), lambda b, g, pt: (b, 0, 0)),
                  pl.BlockSpec((nh, dl, dv), lambda b, g, pt: (0, 0, 0))]
                 + _page_specs(layer, n_pages, pg, PAGE_SIZE, dl) + _page_specs(layer, n_pages, pg, dr, PAGE_SIZE),
        out_specs=pl.BlockSpec((1, nh, t, dv), lambda b, g, pt: (b, 0, 0, 0)),
        scratch_shapes=[pltpu.VMEM((rows, 1), F32), pltpu.VMEM((rows, 1), F32), pltpu.VMEM((rows, dl), F32)],
    )
    out = pl.pallas_call(
        functools.partial(_smla_kernel, pg=pg, scale=scale),
        grid_spec=grid_spec,
        out_shape=jax.ShapeDtypeStruct((db, nh, t, dv), F32),
        compiler_params=pltpu.CompilerParams(dimension_semantics=("parallel", "arbitrary"),
                                             vmem_limit_bytes=VMEM_LIMIT),
        name="sample_mla_attention",
    )(page_table.reshape(-1), qa, qr, cn, rn, w_uv.transpose(1, 0, 2), *([pool_ckv] * pg), *([pool_ckr] * pg))
    return out.transpose(0, 2, 1, 3).reshape(db, t, nh * dv)


def _pmla_kernel(q_ref, k_ref, wuv_ref, o_ref, m_ref, l_ref, acc_ref, *, tk, dl, scale):
    n_heads, tq, dq = q_ref.shape[1:]
    rows = n_heads * tq
    qb = pl.program_id(1)
    q = q_ref[0].reshape(rows, dq)
    m_ref[...] = jnp.full(m_ref.shape, MASK_VALUE, F32)
    l_ref[...] = jnp.zeros(l_ref.shape, F32)
    acc_ref[...] = jnp.zeros(acc_ref.shape, F32)
    qpos = qb * tq + (lax.broadcasted_iota(jnp.int32, (rows, tk), 0) & (tq - 1))
    kofs = lax.broadcasted_iota(jnp.int32, (rows, tk), 1)

    def body(c, carry):
        start = pl.multiple_of(c * tk, tk)
        kc = k_ref[0, pl.ds(start, tk), :]
        s = lax.dot_general(q, kc, _NT, preferred_element_type=F32) * scale
        s = jnp.where(start + kofs <= qpos, s, MASK_VALUE)
        m_old = m_ref[...]
        m_new = jnp.maximum(m_old, jnp.max(s, axis=1, keepdims=True))
        alpha = jnp.exp(m_old - m_new)
        p = jnp.exp(s - m_new)
        l_ref[...] = alpha * l_ref[...] + jnp.sum(p, axis=1, keepdims=True)
        acc_ref[...] = alpha * acc_ref[...] + jnp.dot(p.astype(BF16), kc[:, :dl], preferred_element_type=F32)
        m_ref[...] = m_new
        return carry

    lax.fori_loop(0, (qb * tq) // tk + 1, body, 0)
    o_lat = (acc_ref[...] / l_ref[...]).astype(BF16)
    for h in range(n_heads):
        o_ref[0, h] = jnp.dot(o_lat[h * tq:(h + 1) * tq], wuv_ref[h].astype(BF16), preferred_element_type=F32)


def mla_prompt(q_abs, q_rope, c_kv, k_rope, w_uv, tq=128, tk=512):
    b, l, nh, dl = q_abs.shape
    dr, dv = q_rope.shape[3], w_uv.shape[2]
    tk = min(tk, l)
    assert tq & (tq - 1) == 0 and l % tk == 0 and tk % tq == 0
    qcat = jnp.concatenate([q_abs, q_rope], -1).transpose(0, 2, 1, 3).astype(BF16)
    kcat = jnp.concatenate([c_kv, k_rope], -1).astype(BF16)
    out = pl.pallas_call(
        functools.partial(_pmla_kernel, tk=tk, dl=dl, scale=(C_NOPE + C_ROPE) ** -0.5),
        grid=(b, l // tq),
        in_specs=[pl.BlockSpec((1, nh, tq, dl + dr), lambda bi, qb: (bi, 0, qb, 0)),
                  pl.BlockSpec((1, l, dl + dr), lambda bi, qb: (bi, 0, 0)),
                  pl.BlockSpec((nh, dl, dv), lambda bi, qb: (0, 0, 0))],
        out_specs=pl.BlockSpec((1, nh, tq, dv), lambda bi, qb: (bi, 0, qb, 0)),
        out_shape=jax.ShapeDtypeStruct((b, nh, l, dv), F32),
        scratch_shapes=[pltpu.VMEM((nh * tq, 1), F32), pltpu.VMEM((nh * tq, 1), F32),
                        pltpu.VMEM((nh * tq, dl), F32)],
        compiler_params=pltpu.CompilerParams(dimension_semantics=("parallel", "arbitrary"),
                                             vmem_limit_bytes=VMEM_LIMIT),
        name="mla_prompt",
    )(qcat, kcat, w_uv.transpose(1, 0, 2))
    return out.transpose(0, 2, 1, 3).reshape(b, l, nh * dv)


def prompt_mixers(h, pos, w_in, lb, a_norm_g, cq_g, w_uq, ckv_g, w_uk, w_uv):
    y = _mm3(h, w_in)
    (b_q, b_k, b_v, b_qi, b_ki, b_w), (c_qn, c_qr, c_kv, c_kr) = project(y, pos, cq_g, w_uq, ckv_g)
    s0 = jnp.zeros((h.shape[0], A_HEADS, A_DK, A_DV), F32)
    o_a, s_a = hgrn2_mixer(y, lb.reshape(A_HEADS, A_DK), a_norm_g, s0, A_HEADS, A_DK)
    o_b = dsa_prompt(b_q, b_k, b_v, b_qi, b_ki, b_w, pos)
    q_abs = jnp.einsum('bthn,rhn->bthr', c_qn, w_uk)
    o_c = mla_prompt(q_abs, c_qr, c_kv, c_kr, w_uv)
    mix = jnp.concatenate([o_a, o_b, o_c], -1)
    return mix, (b_k, b_v, b_ki, c_kv, c_kr, s_a)


def sample_mixers(h, pos, past, layer, pool_bk, pool_bv, pool_bidx, pool_ckv, pool_ckr, s_prev, page_table,
                  w_in, lb, a_norm_g, cq_g, w_uq, ckv_g, w_uk, w_uv):
    y = _mm3(h, w_in)
    (b_q, b_k, b_v, b_qi, b_ki, b_w), (c_qn, c_qr, c_kv, c_kr) = project(y, pos, cq_g, w_uq, ckv_g)
    o_a, s_a = hgrn2_mixer(y, lb.reshape(A_HEADS, A_DK), a_norm_g, s_prev, A_HEADS, A_DK)
    o_b = dsa_sample(b_q, b_k, b_v, b_qi, b_ki, b_w, past, pool_bk, pool_bv, pool_bidx, layer, page_table)
    q_abs = jnp.einsum('bthn,rhn->bthr', c_qn, w_uk)
    o_c = sample_mla_attention(q_abs, c_qr, c_kv, c_kr, w_uv, pool_ckv, pool_ckr, layer, page_table,
                               PAGES_PER_STEP, (C_NOPE + C_ROPE) ** -0.5)
    mix = jnp.concatenate([o_a, o_b, o_c], -1)
    return mix, (b_k, b_v, b_ki, c_kv, c_kr, s_a)


def finish_layer(x, mix, w_o, ln1_g, ln1_b, w_gate, w_up, w_down, ln2_g, ln2_b):
    b, l, d = x.shape
    x1, x1b = matmul_res_ln(mix.reshape(b * l, d).astype(BF16), w_o, x.reshape(b * l, d), ln1_g, ln1_b)
    hmid = ffn_up(x1b, w_gate, w_up)
    x2, x2b = matmul_res_ln(hmid, w_down, x1, ln2_g, ln2_b)
    return x2.reshape(b, l, d), x2b.reshape(b, l, d)


def kernel(x_prompt, x_sample, cache_b_k, cache_b_v, cache_b_idx, cache_c_kv, cache_c_kr, state_a,
           page_table, ln_in_g, ln_in_b, w_in, hgrn_lb_logits, hgrn_norm_g, mla_q_norm_g, mla_w_uq,
           mla_kv_norm_g, mla_w_uk, mla_w_uv, w_o, ln1_g, ln1_b, w_gate, w_up, w_down, ln2_g, ln2_b):
    lower_bounds = hgrn_lower_bounds(hgrn_lb_logits)
    past = page_table.shape[1] * PAGE_SIZE
    pos_p = jnp.arange(x_prompt.shape[1], dtype=jnp.int32)
    pos_s = past + jnp.arange(x_sample.shape[1], dtype=jnp.int32)
    xp = layer_norm(x_prompt, ln_in_g, ln_in_b)
    xs = layer_norm(x_sample, ln_in_g, ln_in_b)
    kv_width = B_KV_HEADS * B_HEAD_DIM
    pool_bk = cache_b_k.transpose(0, 1, 3, 4, 2).reshape(cache_b_k.shape[:2] + (kv_width, PAGE_SIZE))
    pool_bv = cache_b_v.transpose(0, 1, 3, 4, 2).reshape(cache_b_v.shape[:2] + (kv_width, PAGE_SIZE))
    pool_bidx = cache_b_idx.transpose(0, 1, 3, 2)
    pool_ckr = cache_c_kr.transpose(0, 1, 3, 2)
    w_in_b, w_o_b, w_gate_b, w_up_b, w_down_b = (w.astype(BF16) for w in (w_in, w_o, w_gate, w_up, w_down))
    xpb, xsb = xp.astype(BF16), xs.astype(BF16)
    new_p = ([], [], [], [], [], [])
    new_s = ([], [], [], [], [], [])
    for l in range(DEPTH):
        mix_w = (w_in_b[l], lower_bounds[l], hgrn_norm_g[l], mla_q_norm_g[l], mla_w_uq[l],
                 mla_kv_norm_g[l], mla_w_uk[l], mla_w_uv[l])
        ffn_w = (w_o_b[l], ln1_g[l], ln1_b[l], w_gate_b[l], w_up_b[l], w_down_b[l], ln2_g[l], ln2_b[l])
        mix_p, st_p = prompt_mixers(xpb, pos_p, *mix_w)
        mix_s, st_s = sample_mixers(xsb, pos_s, past, l, pool_bk, pool_bv, pool_bidx,
                                    cache_c_kv, pool_ckr, state_a[l], page_table, *mix_w)
        xp, xpb = finish_layer(xp, mix_p, *ffn_w)
        xs, xsb = finish_layer(xs, mix_s, *ffn_w)
        for lst, st in zip(new_p, st_p):
            lst.append(st)
        for lst, st in zip(new_s, st_s):
            lst.append(st)
    b_k_p, b_v_p, b_idx_p, c_kv_p, c_kr_p, s_a_p = [jnp.stack(v, 0) for v in new_p]
    b_k_s, b_v_s, b_idx_s, c_kv_s, c_kr_s, s_a_s = [jnp.stack(v, 0) for v in new_s]
    return (xp, xs, b_k_p, b_v_p, b_idx_p, c_kv_p, c_kr_p, s_a_p,
            b_k_s, b_v_s, b_idx_s, c_kv_s, c_kr_s, s_a_s)
```

```python
import functools

import jax
import jax.numpy as jnp
import numpy as np
from jax import lax
from jax.experimental import pallas as pl
from jax.experimental.pallas import tpu as pltpu

D_MODEL = 1024
DEPTH = 4
PAGE_SIZE = 128
A_HEADS = 4
A_DK = 128
A_DV = 128
B_HEADS = 4
B_KV_HEADS = 2
B_HEAD_DIM = 64
IDX_HEADS = 8
IDX_DIM = 64
TOPK_MAX = 256
C_HEADS = 4
C_NOPE = 64
C_ROPE = 32
C_V = 64
C_Q_LORA = 256
C_KV_LORA = 128
ROPE_THETA = 10000.0
EPS = 1e-5
MASK_VALUE = -1e30
DEEPNORM_ALPHA = (2 * DEPTH) ** 0.25
IN_SPLITS = (A_HEADS * A_DK, A_HEADS * A_DK, A_HEADS * A_DV, A_HEADS * A_DV,
             B_HEADS * B_HEAD_DIM, B_KV_HEADS * B_HEAD_DIM, B_KV_HEADS * B_HEAD_DIM,
             IDX_HEADS * IDX_DIM, IDX_DIM, IDX_HEADS,
             C_Q_LORA, C_KV_LORA, C_ROPE)
F32 = jnp.float32
BF16 = jnp.bfloat16


def _mm_kernel(x_ref, w_ref, o_ref):
    o_ref[...] = jnp.dot(x_ref[...], w_ref[...], preferred_element_type=F32)


def _matmul(x, w, tm=1024, tn=512):
    m, k = x.shape
    n = w.shape[1]
    tm = min(tm, m)
    tn = min(tn, n)
    return pl.pallas_call(
        _mm_kernel,
        grid=(pl.cdiv(m, tm), pl.cdiv(n, tn)),
        in_specs=[pl.BlockSpec((tm, k), lambda i, j: (i, 0)),
                  pl.BlockSpec((k, tn), lambda i, j: (0, j))],
        out_specs=pl.BlockSpec((tm, tn), lambda i, j: (i, j)),
        out_shape=jax.ShapeDtypeStruct((m, n), F32),
        compiler_params=pltpu.CompilerParams(dimension_semantics=("parallel", "parallel"),
                                             vmem_limit_bytes=48 * 1024 * 1024),
        name="dense_matmul",
    )(x, w)


def _mm3(h, w):
    b, l, d = h.shape
    return _matmul(h.reshape(b * l, d), w).reshape(b, l, w.shape[1])


def _ffn_up_kernel(x_ref, wg_ref, wu_ref, h_ref):
    x = x_ref[...]
    g = jnp.dot(x, wg_ref[...], preferred_element_type=F32)
    u = jnp.dot(x, wu_ref[...], preferred_element_type=F32)
    h_ref[...] = (g * jax.nn.sigmoid(g) * u).astype(BF16)


def ffn_up(x, w_gate, w_up, tm=512):
    m, d = x.shape
    f = w_gate.shape[1]
    tm = min(tm, m)
    tn = f // 2
    assert m % tm == 0 and tn % 128 == 0
    return pl.pallas_call(
        _ffn_up_kernel,
        grid=(m // tm, 2),
        in_specs=[pl.BlockSpec((tm, d), lambda i, j: (i, 0)),
                  pl.BlockSpec((d, tn), lambda i, j: (0, j)),
                  pl.BlockSpec((d, tn), lambda i, j: (0, j))],
        out_specs=pl.BlockSpec((tm, tn), lambda i, j: (i, j)),
        out_shape=jax.ShapeDtypeStruct((m, f), BF16),
        compiler_params=pltpu.CompilerParams(dimension_semantics=("parallel", "parallel"),
                                             vmem_limit_bytes=48 * 1024 * 1024),
        name="ffn_up",
    )(x, w_gate, w_up)


def _mm_res_ln_kernel(a_ref, w_ref, res_ref, g_ref, b_ref, o_ref, ob_ref):
    y = DEEPNORM_ALPHA * res_ref[...] + jnp.dot(a_ref[...], w_ref[...], preferred_element_type=F32)
    mu = jnp.mean(y, axis=-1, keepdims=True)
    var = jnp.mean(jnp.square(y - mu), axis=-1, keepdims=True)
    out = (y - mu) * lax.rsqrt(var + EPS) * g_ref[...] + b_ref[...]
    o_ref[...] = out
    ob_ref[...] = out.astype(BF16)


def matmul_res_ln(a, w, res, g, b, tm=512):
    m, k = a.shape
    d = w.shape[1]
    tm = min(tm, m)
    assert m % tm == 0
    return pl.pallas_call(
        _mm_res_ln_kernel,
        grid=(m // tm,),
        in_specs=[pl.BlockSpec((tm, k), lambda i: (i, 0)),
                  pl.BlockSpec((k, d), lambda i: (0, 0)),
                  pl.BlockSpec((tm, d), lambda i: (i, 0)),
                  pl.BlockSpec((1, d), lambda i: (0, 0)),
                  pl.BlockSpec((1, d), lambda i: (0, 0))],
        out_specs=[pl.BlockSpec((tm, d), lambda i: (i, 0)), pl.BlockSpec((tm, d), lambda i: (i, 0))],
        out_shape=[jax.ShapeDtypeStruct((m, d), F32), jax.ShapeDtypeStruct((m, d), BF16)],
        compiler_params=pltpu.CompilerParams(dimension_semantics=("parallel",),
                                             vmem_limit_bytes=48 * 1024 * 1024),
        name="matmul_res_ln",
    )(a, w, res, g.reshape(1, d), b.reshape(1, d))


def layer_norm(x, g, b):
    mu = jnp.mean(x, -1, keepdims=True)
    var = jnp.mean(jnp.square(x - mu), -1, keepdims=True)
    return (x - mu) * lax.rsqrt(var + EPS) * g + b


def rms_norm(x, g):
    return x * lax.rsqrt(jnp.mean(jnp.square(x), -1, keepdims=True) + EPS) * g


def rope(x, pos):
    half = x.shape[-1] // 2
    inv = ROPE_THETA ** (-jnp.arange(half, dtype=F32) / half)
    ang = pos.astype(F32)[:, None] * inv[None, :]
    shape = (ang.shape[0],) + (1,) * (x.ndim - 3) + (half,)
    cos = jnp.cos(ang).reshape(shape)
    sin = jnp.sin(ang).reshape(shape)
    x1, x2 = x[..., :half], x[..., half:]
    return jnp.concatenate([x1 * cos - x2 * sin, x2 * cos + x1 * sin], -1)


def hgrn_lower_bounds(logits):
    p = jax.nn.softmax(logits.astype(F32), axis=0)
    return jnp.clip(jnp.cumsum(p, axis=0) - p[0:1], 0.0, 1.0 - 1e-4)


def project(y, pos, cq_g, w_uq, ckv_g):
    b, l = y.shape[:2]
    offs = [int(o) for o in np.cumsum(IN_SPLITS)[:-1]]
    _, _, _, _, bq, bk, bv, bqi, bki, bw, cdq, cdkv, ckr = jnp.split(y, offs, axis=-1)
    b_q = rope(bq.reshape(b, l, B_HEADS, B_HEAD_DIM), pos)
    b_k = rope(bk.reshape(b, l, B_KV_HEADS, B_HEAD_DIM), pos)
    b_v = bv.reshape(b, l, B_KV_HEADS, B_HEAD_DIM)
    b_qi = rope(bqi.reshape(b, l, IDX_HEADS, IDX_DIM), pos)
    b_ki = rope(bki[:, :, None, :], pos)[:, :, 0]
    b_w = bw * (IDX_HEADS ** -0.5 * IDX_DIM ** -0.5)
    cq = jnp.einsum('blr,rhd->blhd', rms_norm(cdq, cq_g), w_uq)
    c_qn = cq[..., :C_NOPE]
    c_qr = rope(cq[..., C_NOPE:], pos)
    c_kv = rms_norm(cdkv, ckv_g)
    c_kr = rope(ckr[:, :, None, :], pos)[:, :, 0]
    return (b_q, b_k, b_v, b_qi, b_ki, b_w), (c_qn, c_qr, c_kv, c_kr)


_TN =(((0,), (0,)), ((), ()))
HGRN_CHUNK = 64
HGRN_SUB = 8


def _split3(x):
    hi = x.astype(BF16)
    r = x - hi.astype(F32)
    mid = r.astype(BF16)
    return hi, mid, (r - mid.astype(F32)).astype(BF16)


def _hgrn_kernel(q_ref, z_ref, v_ref, g_ref, lb_ref, ng_ref, s0_ref, o_ref, sfin_ref, st_ref):
    c = pl.program_id(1)

    @pl.when(c == 0)
    def _():
        st_ref[...] = s0_ref[0]

    n = q_ref.shape[1]
    n_heads, _, dk = lb_ref.shape
    sb = min(HGRN_SUB, n)
    tri = lax.broadcasted_iota(jnp.int32, (n, n), 0) >= lax.broadcasted_iota(jnp.int32, (n, n), 1)
    tri = jnp.where(tri, 1.0, 0.0).astype(BF16)
    srow = lax.broadcasted_iota(jnp.int32, (sb, dk), 0)
    for h in range(n_heads):
        cols = slice(h * dk, (h + 1) * dk)
        q, v = q_ref[0, :, cols], v_ref[0, :, cols]
        lb = lb_ref[h]
        f = lb + (1.0 - lb) * jax.nn.sigmoid(z_ref[0, :, cols])
        logf = jnp.log(jnp.maximum(f, 1e-30))
        k = 1.0 - f
        cum = sum(jnp.dot(tri, piece, preferred_element_type=F32) for piece in _split3(logf))
        st = st_ref[h]
        o_inter = lax.dot_general((q * jnp.exp(cum)).astype(BF16), st.astype(BF16), _NT,
                                  preferred_element_type=F32)
        blocks = []
        for i in range(n // sb):
            lo = i * sb
            cum_i, q_i, k_i, v_i = cum[lo:lo + sb], q[lo:lo + sb], k[lo:lo + sb], v[lo:lo + sb]
            o_i = jnp.zeros((sb, dk), F32)
            if i > 0:
                start = cum[lo - 1:lo]
                qs = (q_i * jnp.exp(cum_i - start)).astype(BF16)
                ks = (k[:lo] * jnp.exp(start - cum[:lo])).astype(BF16)
                a = lax.dot_general(qs, ks, _NT, preferred_element_type=F32)
                o_i = jnp.dot(a.astype(BF16), v[:lo].astype(BF16), preferred_element_type=F32)
            for t in range(sb):
                decay = jnp.exp(jnp.where(srow <= t, cum_i[t:t + 1] - cum_i, MASK_VALUE))
                a_t = jnp.sum(q_i[t:t + 1] * k_i * decay, axis=1, keepdims=True)
                row = jnp.sum(a_t * v_i, axis=0, keepdims=True)
                o_i = o_i + jnp.where(srow == t, row, 0.0)
            blocks.append(o_i)
        o = o_inter + (jnp.concatenate(blocks, axis=0) if len(blocks) > 1 else blocks[0])
        last = cum[n - 1:n]
        kd = (k * jnp.exp(last - cum)).astype(BF16)
        st_new = jnp.exp(last) * st + lax.dot_general(v.astype(BF16), kd, _TN, preferred_element_type=F32)
        st_ref[h] = st_new
        sfin_ref[0, h] = st_new
        o = o * lax.rsqrt(jnp.mean(jnp.square(o), axis=-1, keepdims=True) + EPS) * ng_ref[h]
        g = g_ref[0, :, cols]
        o_ref[0, :, cols] = o * (g * jax.nn.sigmoid(g))


def hgrn2_mixer(y, lb, norm_g, s0, n_heads, dk):
    b, l, _ = y.shape
    c = min(HGRN_CHUNK, l)
    width = n_heads * dk
    assert l % c == 0

    def col(j):
        return pl.BlockSpec((1, c, width), lambda bi, ci: (bi, ci, j))

    vec = pl.BlockSpec((n_heads, 1, dk), lambda bi, ci: (0, 0, 0))
    state = pl.BlockSpec((1, n_heads, dk, dk), lambda bi, ci: (bi, 0, 0, 0))
    o, s_t = pl.pallas_call(
        _hgrn_kernel,
        grid=(b, l // c),
        in_specs=[col(0), col(1), col(2), col(3), vec, vec, state],
        out_specs=[pl.BlockSpec((1, c, width), lambda bi, ci: (bi, ci, 0)), state],
        out_shape=[jax.ShapeDtypeStruct((b, l, width), F32),
                   jax.ShapeDtypeStruct((b, n_heads, dk, dk), F32)],
        scratch_shapes=[pltpu.VMEM((n_heads, dk, dk), F32)],
        compiler_params=pltpu.CompilerParams(dimension_semantics=("parallel", "arbitrary"),
                                             vmem_limit_bytes=48 * 1024 * 1024),
        name="hgrn2_mixer",
    )(y, y, y, y, lb.reshape(n_heads, 1, dk), norm_g.reshape(n_heads, 1, dk), s0.transpose(0, 1, 3, 2))
    return o, s_t.transpose(0, 1, 3, 2)


INT_MIN = -2 ** 31
_NT = (((1,), (1,)), ((), ()))


REDUCE_LANES = 64
KEY_STEP = 512


def _reduce_rows(x, op):
    rows = x.shape[0]
    if rows > REDUCE_LANES and rows % REDUCE_LANES == 0:
        x = op(x.reshape(rows // REDUCE_LANES, REDUCE_LANES, x.shape[1]), axis=0)
    return op(x, axis=0, keepdims=True)


def _dsa_prompt_body(lk, qi_ref, wT_ref, ki_ref, q_ref, k_ref, vT_ref, o_ref, key_ref, sel_ref, *, topk, scale):
    n_idx, tq = qi_ref.shape[1], qi_ref.shape[2]
    qb = pl.program_id(1)
    ki = ki_ref[0, :lk].astype(BF16)
    acc = jnp.zeros((lk, tq), F32)
    for h in range(n_idx):
        d = lax.dot_general(ki, qi_ref[0, h].astype(BF16), _NT, preferred_element_type=F32)
        acc = acc + wT_ref[0, h:h + 1, :] * jnp.maximum(d, 0.0)
    kpos = lax.broadcasted_iota(jnp.int32, (lk, tq), 0)
    qpos = qb * tq + lax.broadcasted_iota(jnp.int32, (lk, tq), 1)
    causal = kpos <= qpos
    s = jnp.where(causal, acc, MASK_VALUE)
    s = jnp.where(s == 0.0, 0.0, s)
    bits = pltpu.bitcast(s, jnp.int32)
    key_ref[:lk] = bits ^ ((bits >> 31) & 0x7FFFFFFF)

    def count_ge(cand):
        return _reduce_rows(jnp.where(key_ref[:lk] >= cand, 1.0, 0.0), jnp.sum)

    kf = float(topk)
    t0 = jnp.where(count_ge(jnp.zeros((1, tq), jnp.int32)) >= kf, 0, INT_MIN).astype(jnp.int32)

    def step(i, t):
        cand = t | jnp.left_shift(jnp.int32(1), 30 - i)
        return jnp.where(count_ge(cand) >= kf, cand, t)

    thr = lax.fori_loop(0, 31, step, t0)
    need = kf - _reduce_rows(jnp.where(key_ref[:lk] > thr, 1.0, 0.0), jnp.sum)
    tri = (lax.broadcasted_iota(jnp.int32, (128, 128), 0) >= lax.broadcasted_iota(jnp.int32, (128, 128), 1))
    tri = jnp.where(tri, 1.0, 0.0).astype(BF16)
    offset = jnp.zeros((1, tq), F32)
    for c in range(lk // 128):
        key_c = key_ref[c * 128:(c + 1) * 128, :]
        tie = key_c == thr
        pref = jnp.dot(tri, jnp.where(tie, 1.0, 0.0).astype(BF16), preferred_element_type=F32) + offset
        offset = pref[127:128, :]
        chosen = jnp.where(key_c > thr, 1.0, jnp.where(tie, jnp.where(pref <= need, 1.0, 0.0), 0.0))
        sel_ref[c * 128:(c + 1) * 128, :] = jnp.where(causal[c * 128:(c + 1) * 128, :], chosen, 0.0)

    n_kv = k_ref.shape[1]
    grp = q_ref.shape[1] // n_kv
    for j in range(n_kv):
        kj = k_ref[0, j, :lk].astype(BF16)
        vTj = vT_ref[0, j, :, :lk].astype(BF16)
        for g in range(grp):
            h = j * grp + g
            sT = lax.dot_general(kj, q_ref[0, h].astype(BF16), _NT, preferred_element_type=F32) * scale
            sT = jnp.where(sel_ref[:lk] > 0.0, sT, MASK_VALUE)
            m = _reduce_rows(sT, jnp.max)
            p = jnp.exp(sT - m)
            l = _reduce_rows(p, jnp.sum)
            oT = jnp.dot(vTj, p.astype(BF16), preferred_element_type=F32)
            o_ref[0, h] = oT / l


def _dsa_prompt_kernel(qi_ref, wT_ref, ki_ref, *rest, topk, scale):
    tq, L = qi_ref.shape[2], ki_ref.shape[1]
    step = min(KEY_STEP, L)
    assert L % step == 0 and step % tq == 0 and step >= topk
    extent = (pl.program_id(1) * tq) // step
    for i in range(L // step):
        pl.when(extent == i)(functools.partial(_dsa_prompt_body, (i + 1) * step, qi_ref, wT_ref, ki_ref, *rest,
                                               topk=topk, scale=scale))


def dsa_prompt(q, k, v, qi, ki, w, pos, tq=128):
    b, l, nh, dh = q.shape
    n_idx, n_kv = qi.shape[2], k.shape[2]
    topk = min(TOPK_MAX, l // 4)
    out = pl.pallas_call(
        functools.partial(_dsa_prompt_kernel, topk=topk, scale=dh ** -0.5),
        grid=(b, l // tq),
        in_specs=[
            pl.BlockSpec((1, n_idx, tq, qi.shape[3]), lambda bi, qb: (bi, 0, qb, 0)),
            pl.BlockSpec((1, n_idx, tq), lambda bi, qb: (bi, 0, qb)),
            pl.BlockSpec((1, l, ki.shape[2]), lambda bi, qb: (bi, 0, 0)),
            pl.BlockSpec((1, nh, tq, dh), lambda bi, qb: (bi, 0, qb, 0)),
            pl.BlockSpec((1, n_kv, l, dh), lambda bi, qb: (bi, 0, 0, 0)),
            pl.BlockSpec((1, n_kv, dh, l), lambda bi, qb: (bi, 0, 0, 0)),
        ],
        out_specs=pl.BlockSpec((1, nh, dh, tq), lambda bi, qb: (bi, 0, 0, qb)),
        out_shape=jax.ShapeDtypeStruct((b, nh, dh, l), F32),
        scratch_shapes=[pltpu.VMEM((l, tq), jnp.int32), pltpu.VMEM((l, tq), F32)],
        compiler_params=pltpu.CompilerParams(dimension_semantics=("parallel", "arbitrary"),
                                             vmem_limit_bytes=48 * 1024 * 1024),
        name="dsa_prompt",
    )(qi.transpose(0, 2, 1, 3), w.transpose(0, 2, 1), ki, q.transpose(0, 2, 1, 3),
      k.transpose(0, 2, 1, 3), v.transpose(0, 2, 3, 1))
    return out.transpose(0, 3, 1, 2).reshape(b, l, nh * dh)


VMEM_LIMIT = 48 * 1024 * 1024
PAGES_PER_STEP = 64


def _page_specs(layer, n_pages, pg, rows, cols):
    def spec(j):
        return pl.BlockSpec((1, 1, rows, cols),
                            lambda b, g, pt: (layer, pt[b * n_pages + g * pg + j], 0, 0))
    return [spec(j) for j in range(pg)]


def _tokens_last(x_new):
    t = x_new.shape[1]
    return jnp.pad(x_new.transpose(0, 2, 1), ((0, 0), (0, 0), (0, PAGE_SIZE - t)))


def _sidx_kernel(pt_ref, qi_ref, w_ref, kn_ref, *rest, pg):
    page_refs, (sc_ref, scn_ref) = rest[:pg], rest[pg:]
    n_tok = sc_ref.shape[1]
    qi = qi_ref[0].astype(BF16)
    w = w_ref[0]

    def score(keys_t):
        d = jnp.dot(qi, keys_t.astype(BF16), preferred_element_type=F32)
        r = w * jnp.maximum(d, 0.0)
        acc = jnp.zeros((n_tok, PAGE_SIZE), F32)
        for h in range(r.shape[0] // n_tok):
            acc = acc + r[h * n_tok:(h + 1) * n_tok]
        return acc

    for j in range(pg):
        sc_ref[0, :, j * PAGE_SIZE:(j + 1) * PAGE_SIZE] = score(page_refs[j][0, 0])
    row = lax.broadcasted_iota(jnp.int32, (n_tok, PAGE_SIZE), 0)
    col = lax.broadcasted_iota(jnp.int32, (n_tok, PAGE_SIZE), 1)
    scn_ref[0] = jnp.where(col <= row, score(kn_ref[0]), MASK_VALUE)


def sample_indexer_scores(qi, w, ki_new, pool_idx, layer, page_table, pg):
    db, t, nh, d = qi.shape
    n_pages = page_table.shape[1]
    qi_s = qi.transpose(0, 2, 1, 3).reshape(db, nh * t, d)
    w_s = w.transpose(0, 2, 1).reshape(db, nh * t, 1)
    kn = _tokens_last(ki_new)
    grid_spec = pltpu.PrefetchScalarGridSpec(
        num_scalar_prefetch=1,
        grid=(db, n_pages // pg),
        in_specs=[pl.BlockSpec((1, nh * t, d), lambda b, g, pt: (b, 0, 0)),
                  pl.BlockSpec((1, nh * t, 1), lambda b, g, pt: (b, 0, 0)),
                  pl.BlockSpec((1, d, PAGE_SIZE), lambda b, g, pt: (b, 0, 0))]
                 + _page_specs(layer, n_pages, pg, d, PAGE_SIZE),
        out_specs=[pl.BlockSpec((1, t, pg * PAGE_SIZE), lambda b, g, pt: (b, 0, g)),
                   pl.BlockSpec((1, t, PAGE_SIZE), lambda b, g, pt: (b, 0, 0))],
    )
    return pl.pallas_call(
        functools.partial(_sidx_kernel, pg=pg),
        grid_spec=grid_spec,
        out_shape=[jax.ShapeDtypeStruct((db, t, n_pages * PAGE_SIZE), F32),
                   jax.ShapeDtypeStruct((db, t, PAGE_SIZE), F32)],
        compiler_params=pltpu.CompilerParams(dimension_semantics=("parallel", "arbitrary"),
                                             vmem_limit_bytes=VMEM_LIMIT),
        name="sample_indexer",
    )(page_table.reshape(-1), qi_s, w_s, kn, *([pool_idx] * pg))


def _ssel_kernel(sc_ref, scn_ref, sel_ref, seln_ref, key_ref, *, topk):
    g, t, p = sc_ref.shape
    rows = g * t
    n_chunks = p // PAGE_SIZE + 1

    def to_key(s):
        s = jnp.where(s == 0.0, 0.0, s)
        bits = pltpu.bitcast(s, jnp.int32)
        return bits ^ ((bits >> 31) & 0x7FFFFFFF)

    key_ref[:, :p] = to_key(sc_ref[...].reshape(rows, p))
    key_ref[:, p:] = to_key(scn_ref[...].reshape(rows, PAGE_SIZE))

    def count_ge(cand):
        return jnp.sum(jnp.where(key_ref[...] >= cand, 1.0, 0.0), axis=1, keepdims=True)

    kf = float(topk)
    t0 = jnp.where(count_ge(jnp.zeros((rows, 1), jnp.int32)) >= kf, 0, INT_MIN).astype(jnp.int32)

    def step(i, thr):
        cand = thr | jnp.left_shift(jnp.int32(1), 30 - i)
        return jnp.where(count_ge(cand) >= kf, cand, thr)

    thr = lax.fori_loop(0, 31, step, t0)
    need = kf - jnp.sum(jnp.where(key_ref[...] > thr, 1.0, 0.0), axis=1, keepdims=True)
    tri = (lax.broadcasted_iota(jnp.int32, (PAGE_SIZE, PAGE_SIZE), 0)
           <= lax.broadcasted_iota(jnp.int32, (PAGE_SIZE, PAGE_SIZE), 1))
    tri = jnp.where(tri, 1.0, 0.0).astype(BF16)
    offset = jnp.zeros((rows, 1), F32)
    trow = lax.broadcasted_iota(jnp.int32, (rows, PAGE_SIZE), 0) & (t - 1)
    tcol = lax.broadcasted_iota(jnp.int32, (rows, PAGE_SIZE), 1)
    for c in range(n_chunks):
        key_c = key_ref[:, c * PAGE_SIZE:(c + 1) * PAGE_SIZE]
        tie = key_c == thr
        pref = jnp.dot(jnp.where(tie, 1.0, 0.0).astype(BF16), tri, preferred_element_type=F32) + offset
        offset = pref[:, PAGE_SIZE - 1:PAGE_SIZE]
        chosen = jnp.where(key_c > thr, 1.0, jnp.where(tie, jnp.where(pref <= need, 1.0, 0.0), 0.0))
        if c < n_chunks - 1:
            sel_ref[:, :, c * PAGE_SIZE:(c + 1) * PAGE_SIZE] = chosen.reshape(g, t, PAGE_SIZE)
        else:
            seln_ref[...] = jnp.where(tcol <= trow, chosen, 0.0).reshape(g, t, PAGE_SIZE)


def sample_select(sc, scn, topk, g=8):
    db, t, p = sc.shape
    assert t & (t - 1) == 0 and db % g == 0
    return pl.pallas_call(
        functools.partial(_ssel_kernel, topk=topk),
        grid=(db // g,),
        in_specs=[pl.BlockSpec((g, t, p), lambda i: (i, 0, 0)),
                  pl.BlockSpec((g, t, PAGE_SIZE), lambda i: (i, 0, 0))],
        out_specs=[pl.BlockSpec((g, t, p), lambda i: (i, 0, 0)),
                   pl.BlockSpec((g, t, PAGE_SIZE), lambda i: (i, 0, 0))],
        out_shape=[jax.ShapeDtypeStruct((db, t, p), F32), jax.ShapeDtypeStruct((db, t, PAGE_SIZE), F32)],
        scratch_shapes=[pltpu.VMEM((g * t, p + PAGE_SIZE), jnp.int32)],
        compiler_params=pltpu.CompilerParams(dimension_semantics=("parallel",), vmem_limit_bytes=VMEM_LIMIT),
        name="sample_select",
    )(sc, scn)


def _online_softmax_step(s, valid, v_list, m_ref, l_ref, acc_ref, values_tokens_last=False):
    m_old = m_ref[...]
    m_new = jnp.maximum(m_old, jnp.max(s, axis=1, keepdims=True))
    alpha = jnp.exp(m_old - m_new)
    p = jnp.exp(s - m_new)
    if valid is not None:
        p = jnp.where(valid, p, 0.0)
    l_ref[...] = alpha * l_ref[...] + jnp.sum(p, axis=1, keepdims=True)
    acc = alpha * acc_ref[...]
    pb = p.astype(BF16)
    for j, vj in enumerate(v_list):
        pj = pb[:, j * PAGE_SIZE:(j + 1) * PAGE_SIZE]
        if values_tokens_last:
            acc = acc + lax.dot_general(pj, vj, _NT, preferred_element_type=F32)
        else:
            acc = acc + jnp.dot(pj, vj, preferred_element_type=F32)
    acc_ref[...] = acc
    m_ref[...] = m_new


def _sattn_kernel(pt_ref, q_ref, sel_ref, seln_ref, kn_ref, vn_ref, *rest, pg, scale):
    k_refs, v_refs = rest[:pg], rest[pg:2 * pg]
    o_ref, m_ref, l_ref, acc_ref = rest[2 * pg:]
    gi = pl.program_id(1)
    rep = q_ref.shape[1] // sel_ref.shape[1]
    q = q_ref[0].astype(BF16)

    @pl.when(gi == 0)
    def _():
        m_ref[...] = jnp.full(m_ref.shape, MASK_VALUE, F32)
        l_ref[...] = jnp.zeros(l_ref.shape, F32)
        acc_ref[...] = jnp.zeros(acc_ref.shape, F32)

    def attend(k_list, v_list, sel):
        s = jnp.concatenate([jnp.dot(q, kk, preferred_element_type=F32) for kk in k_list], axis=1)
        valid = jnp.concatenate([sel] * rep, axis=0) > 0.0
        s = jnp.where(valid, s * scale, MASK_VALUE)
        _online_softmax_step(s, valid, v_list, m_ref, l_ref, acc_ref, values_tokens_last=True)

    attend([r[0, 0].astype(BF16) for r in k_refs], [r[0, 0].astype(BF16) for r in v_refs], sel_ref[0])

    @pl.when(gi == pl.num_programs(1) - 1)
    def _():
        attend([kn_ref[0].astype(BF16)], [vn_ref[0].astype(BF16)], seln_ref[0])
        o_ref[0] = acc_ref[...] / l_ref[...]


def sample_sparse_attention(q, k_new, v_new, sel, seln, pool_k, pool_v, layer, page_table, pg):
    db, t, nh, dh = q.shape
    n_kv = k_new.shape[2]
    grp = nh // n_kv
    n_pages = page_table.shape[1]
    rows = nh * t
    qh = q.transpose(0, 2, 1, 3).reshape(db, n_kv, grp * t, dh)
    q_pad = jnp.concatenate(
        [jnp.pad(qh[:, j], ((0, 0), (0, 0), (j * dh, (n_kv - 1 - j) * dh))) for j in range(n_kv)], axis=1)
    kn = _tokens_last(k_new.reshape(db, t, n_kv * dh))
    vn = _tokens_last(v_new.reshape(db, t, n_kv * dh))
    width = n_kv * dh
    grid_spec = pltpu.PrefetchScalarGridSpec(
        num_scalar_prefetch=1,
        grid=(db, n_pages // pg),
        in_specs=[pl.BlockSpec((1, rows, width), lambda b, g, pt: (b, 0, 0)),
                  pl.BlockSpec((1, t, pg * PAGE_SIZE), lambda b, g, pt: (b, 0, g)),
                  pl.BlockSpec((1, t, PAGE_SIZE), lambda b, g, pt: (b, 0, 0)),
                  pl.BlockSpec((1, width, PAGE_SIZE), lambda b, g, pt: (b, 0, 0)),
                  pl.BlockSpec((1, width, PAGE_SIZE), lambda b, g, pt: (b, 0, 0))]
                 + _page_specs(layer, n_pages, pg, width, PAGE_SIZE)
                 + _page_specs(layer, n_pages, pg, width, PAGE_SIZE),
        out_specs=pl.BlockSpec((1, rows, width), lambda b, g, pt: (b, 0, 0)),
        scratch_shapes=[pltpu.VMEM((rows, 1), F32), pltpu.VMEM((rows, 1), F32), pltpu.VMEM((rows, width), F32)],
    )
    out = pl.pallas_call(
        functools.partial(_sattn_kernel, pg=pg, scale=dh ** -0.5),
        grid_spec=grid_spec,
        out_shape=jax.ShapeDtypeStruct((db, rows, width), F32),
        compiler_params=pltpu.CompilerParams(dimension_semantics=("parallel", "arbitrary"),
                                             vmem_limit_bytes=VMEM_LIMIT),
        name="sample_sparse_attention",
    )(page_table.reshape(-1), q_pad, sel, seln, kn, vn, *([pool_k] * pg), *([pool_v] * pg))
    out = out.reshape(db, n_kv, grp, t, n_kv, dh)
    out = jnp.stack([out[:, j, :, :, j] for j in range(n_kv)], axis=1)
    return out.transpose(0, 3, 1, 2, 4).reshape(db, t, nh * dh)


def dsa_sample(q, k_new, v_new, qi, ki_new, w, past, pool_k, pool_v, pool_idx, layer, page_table):
    t = q.shape[1]
    topk = min(TOPK_MAX, (past + t) // 4)
    sc, scn = sample_indexer_scores(qi, w, ki_new, pool_idx, layer, page_table, PAGES_PER_STEP)
    sel, seln = sample_select(sc, scn, topk)
    return sample_sparse_attention(q, k_new, v_new, sel, seln, pool_k, pool_v, layer, page_table, PAGES_PER_STEP)


def _smla_kernel(pt_ref, qa_ref, qr_ref, cn_ref, rn_ref, wuv_ref, *rest, pg, scale):
    c_refs, r_refs = rest[:pg], rest[pg:2 * pg]
    o_ref, m_ref, l_ref, acc_ref = rest[2 * pg:]
    gi = pl.program_id(1)
    n_heads = wuv_ref.shape[0]
    t = qa_ref.shape[1] // n_heads
    qa = qa_ref[0].astype(BF16)
    qr = qr_ref[0].astype(BF16)

    @pl.when(gi == 0)
    def _():
        m_ref[...] = jnp.full(m_ref.shape, MASK_VALUE, F32)
        l_ref[...] = jnp.zeros(l_ref.shape, F32)
        acc_ref[...] = jnp.zeros(acc_ref.shape, F32)

    def attend(c_list, r_list, valid):
        s = jnp.concatenate(
            [lax.dot_general(qa, c, _NT, preferred_element_type=F32)
             + jnp.dot(qr, r, preferred_element_type=F32) for c, r in zip(c_list, r_list)], axis=1)
        s = s * scale
        if valid is not None:
            s = jnp.where(valid, s, MASK_VALUE)
        _online_softmax_step(s, valid, c_list, m_ref, l_ref, acc_ref)

    attend([r[0, 0].astype(BF16) for r in c_refs], [r[0, 0].astype(BF16) for r in r_refs], None)

    @pl.when(gi == pl.num_programs(1) - 1)
    def _():
        rows = qa_ref.shape[1]
        tok = lax.broadcasted_iota(jnp.int32, (rows, PAGE_SIZE), 0) & (t - 1)
        col = lax.broadcasted_iota(jnp.int32, (rows, PAGE_SIZE), 1)
        attend([cn_ref[0].astype(BF16)], [rn_ref[0].astype(BF16)], col <= tok)
        o_lat = (acc_ref[...] / l_ref[...]).astype(BF16)
        for h in range(n_heads):
            o_ref[0, h] = jnp.dot(o_lat[h * t:(h + 1) * t], wuv_ref[h].astype(BF16), preferred_element_type=F32)


def sample_mla_attention(q_abs, q_rope, c_new, r_new, w_uv, pool_ckv, pool_ckr, layer, page_table, pg, scale):
    db, t, nh, dl = q_abs.shape
    assert t & (t - 1) == 0
    dr = q_rope.shape[3]
    dv = w_uv.shape[2]
    n_pages = page_table.shape[1]
    rows = nh * t
    qa = q_abs.transpose(0, 2, 1, 3).reshape(db, rows, dl)
    qr = q_rope.transpose(0, 2, 1, 3).reshape(db, rows, dr)
    cn = jnp.pad(c_new, ((0, 0), (0, PAGE_SIZE - t), (0, 0)))
    rn = _tokens_last(r_new)
    grid_spec = pltpu.PrefetchScalarGridSpec(
        num_scalar_prefetch=1,
        grid=(db, n_pages // pg),
        in_specs=[pl.BlockSpec((1, rows, dl), lambda b, g, pt: (b, 0, 0)),
                  pl.BlockSpec((1, rows, dr), lambda b, g, pt: (b, 0, 0)),
                  pl.BlockSpec((1, PAGE_SIZE, dl), lambda b, g, pt: (b, 0, 0)),
                  pl.BlockSpec((1, dr, PAGE_SIZE), lambda b, g, pt: (b, 0, 0)),
                  pl.BlockSpec((nh, dl, dv), lambda b, g, pt: (0, 0, 0))]
                 + _page_specs(layer, n_pages, pg, PAGE_SIZE, dl) + _page_specs(layer, n_pages, pg, dr, PAGE_SIZE),
        out_specs=pl.BlockSpec((1, nh, t, dv), lambda b, g, pt: (b, 0, 0, 0)),
        scratch_shapes=[pltpu.VMEM((rows, 1), F32), pltpu.VMEM((rows, 1), F32), pltpu.VMEM((rows, dl), F32)],
    )
    out = pl.pallas_call(
        functools.partial(_smla_kernel, pg=pg, scale=scale),
        grid_spec=grid_spec,
        out_shape=jax.ShapeDtypeStruct((db, nh, t, dv), F32),
        compiler_params=pltpu.CompilerParams(dimension_semantics=("parallel", "arbitrary"),
                                             vmem_limit_bytes=VMEM_LIMIT),
        name="sample_mla_attention",
    )(page_table.reshape(-1), qa, qr, cn, rn, w_uv.transpose(1, 0, 2), *([pool_ckv] * pg), *([pool_ckr] * pg))
    return out.transpose(0, 2, 1, 3).reshape(db, t, nh * dv)


def _pmla_kernel(q_ref, k_ref, wuv_ref, o_ref, m_ref, l_ref, acc_ref, *, tk, dl, scale):
    n_heads, tq, dq = q_ref.shape[1:]
    rows = n_heads * tq
    qb = pl.program_id(1)
    q = q_ref[0].reshape(rows, dq)
    m_ref[...] = jnp.full(m_ref.shape, MASK_VALUE, F32)
    l_ref[...] = jnp.zeros(l_ref.shape, F32)
    acc_ref[...] = jnp.zeros(acc_ref.shape, F32)
    qpos = qb * tq + (lax.broadcasted_iota(jnp.int32, (rows, tk), 0) & (tq - 1))
    kofs = lax.broadcasted_iota(jnp.int32, (rows, tk), 1)

    def body(c, carry):
        start = pl.multiple_of(c * tk, tk)
        kc = k_ref[0, pl.ds(start, tk), :]
        s = lax.dot_general(q, kc, _NT, preferred_element_type=F32) * scale
        s = jnp.where(start + kofs <= qpos, s, MASK_VALUE)
        m_old = m_ref[...]
        m_new = jnp.maximum(m_old, jnp.max(s, axis=1, keepdims=True))
        alpha = jnp.exp(m_old - m_new)
        p = jnp.exp(s - m_new)
        l_ref[...] = alpha * l_ref[...] + jnp.sum(p, axis=1, keepdims=True)
        acc_ref[...] = alpha * acc_ref[...] + jnp.dot(p.astype(BF16), kc[:, :dl], preferred_element_type=F32)
        m_ref[...] = m_new
        return carry

    lax.fori_loop(0, (qb * tq) // tk + 1, body, 0)
    o_lat = (acc_ref[...] / l_ref[...]).astype(BF16)
    for h in range(n_heads):
        o_ref[0, h] = jnp.dot(o_lat[h * tq:(h + 1) * tq], wuv_ref[h].astype(BF16), preferred_element_type=F32)


def mla_prompt(q_abs, q_rope, c_kv, k_rope, w_uv, tq=128, tk=512):
    b, l, nh, dl = q_abs.shape
    dr, dv = q_rope.shape[3], w_uv.shape[2]
    tk = min(tk, l)
    assert tq & (tq - 1) == 0 and l % tk == 0 and tk % tq == 0
    qcat = jnp.concatenate([q_abs, q_rope], -1).transpose(0, 2, 1, 3).astype(BF16)
    kcat = jnp.concatenate([c_kv, k_rope], -1).astype(BF16)
    out = pl.pallas_call(
        functools.partial(_pmla_kernel, tk=tk, dl=dl, scale=(C_NOPE + C_ROPE) ** -0.5),
        grid=(b, l // tq),
        in_specs=[pl.BlockSpec((1, nh, tq, dl + dr), lambda bi, qb: (bi, 0, qb, 0)),
                  pl.BlockSpec((1, l, dl + dr), lambda bi, qb: (bi, 0, 0)),
                  pl.BlockSpec((nh, dl, dv), lambda bi, qb: (0, 0, 0))],
        out_specs=pl.BlockSpec((1, nh, tq, dv), lambda bi, qb: (bi, 0, qb, 0)),
        out_shape=jax.ShapeDtypeStruct((b, nh, l, dv), F32),
        scratch_shapes=[pltpu.VMEM((nh * tq, 1), F32), pltpu.VMEM((nh * tq, 1), F32),
                        pltpu.VMEM((nh * tq, dl), F32)],
        compiler_params=pltpu.CompilerParams(dimension_semantics=("parallel", "arbitrary"),
                                             vmem_limit_bytes=VMEM_LIMIT),
        name="mla_prompt",
    )(qcat, kcat, w_uv.transpose(1, 0, 2))
    return out.transpose(0, 2, 1, 3).reshape(b, l, nh * dv)


def prompt_mixers(h, pos, w_in, lb, a_norm_g, cq_g, w_uq, ckv_g, w_uk, w_uv):
    y = _mm3(h, w_in)
    (b_q, b_k, b_v, b_qi, b_ki, b_w), (c_qn, c_qr, c_kv, c_kr) = project(y, pos, cq_g, w_uq, ckv_g)
    s0 = jnp.zeros((h.shape[0], A_HEADS, A_DK, A_DV), F32)
    o_a, s_a = hgrn2_mixer(y, lb.reshape(A_HEADS, A_DK), a_norm_g, s0, A_HEADS, A_DK)
    o_b = dsa_prompt(b_q, b_k, b_v, b_qi, b_ki, b_w, pos)
    q_abs = jnp.einsum('bthn,rhn->bthr', c_qn, w_uk)
    o_c = mla_prompt(q_abs, c_qr, c_kv, c_kr, w_uv)
    mix = jnp.concatenate([o_a, o_b, o_c], -1)
    return mix, (b_k, b_v, b_ki, c_kv, c_kr, s_a)


def sample_mixers(h, pos, past, layer, pool_bk, pool_bv, pool_bidx, pool_ckv, pool_ckr, s_prev, page_table,
                  w_in, lb, a_norm_g, cq_g, w_uq, ckv_g, w_uk, w_uv):
    y = _mm3(h, w_in)
    (b_q, b_k, b_v, b_qi, b_ki, b_w), (c_qn, c_qr, c_kv, c_kr) = project(y, pos, cq_g, w_uq, ckv_g)
    o_a, s_a = hgrn2_mixer(y, lb.reshape(A_HEADS, A_DK), a_norm_g, s_prev, A_HEADS, A_DK)
    o_b = dsa_sample(b_q, b_k, b_v, b_qi, b_ki, b_w, past, pool_bk, pool_bv, pool_bidx, layer, page_table)
    q_abs = jnp.einsum('bthn,rhn->bthr', c_qn, w_uk)
    o_c = sample_mla_attention(q_abs, c_qr, c_kv, c_kr, w_uv, pool_ckv, pool_ckr, layer, page_table,
                               PAGES_PER_STEP, (C_NOPE + C_ROPE) ** -0.5)
    mix = jnp.concatenate([o_a, o_b, o_c], -1)
    return mix, (b_k, b_v, b_ki, c_kv, c_kr, s_a)


def finish_layer(x, mix, w_o, ln1_g, ln1_b, w_gate, w_up, w_down, ln2_g, ln2_b):
    b, l, d = x.shape
    x1, x1b = matmul_res_ln(mix.reshape(b * l, d).astype(BF16), w_o, x.reshape(b * l, d), ln1_g, ln1_b)
    hmid = ffn_up(x1b, w_gate, w_up)
    x2, x2b = matmul_res_ln(hmid, w_down, x1, ln2_g, ln2_b)
    return x2.reshape(b, l, d), x2b.reshape(b, l, d)


def kernel(x_prompt, x_sample, cache_b_k, cache_b_v, cache_b_idx, cache_c_kv, cache_c_kr, state_a,
           page_table, ln_in_g, ln_in_b, w_in, hgrn_lb_logits, hgrn_norm_g, mla_q_norm_g, mla_w_uq,
           mla_kv_norm_g, mla_w_uk, mla_w_uv, w_o, ln1_g, ln1_b, w_gate, w_up, w_down, ln2_g, ln2_b):
    lower_bounds = hgrn_lower_bounds(hgrn_lb_logits)
    past = page_table.shape[1] * PAGE_SIZE
    pos_p = jnp.arange(x_prompt.shape[1], dtype=jnp.int32)
    pos_s = past + jnp.arange(x_sample.shape[1], dtype=jnp.int32)
    xp = layer_norm(x_prompt, ln_in_g, ln_in_b)
    xs = layer_norm(x_sample, ln_in_g, ln_in_b)
    kv_width = B_KV_HEADS * B_HEAD_DIM
    pool_bk = cache_b_k.transpose(0, 1, 3, 4, 2).reshape(cache_b_k.shape[:2] + (kv_width, PAGE_SIZE))
    pool_bv = cache_b_v.transpose(0, 1, 3, 4, 2).reshape(cache_b_v.shape[:2] + (kv_width, PAGE_SIZE))
    pool_bidx = cache_b_idx.transpose(0, 1, 3, 2)
    pool_ckr = cache_c_kr.transpose(0, 1, 3, 2)
    w_in_b, w_o_b, w_gate_b, w_up_b, w_down_b = (w.astype(BF16) for w in (w_in, w_o, w_gate, w_up, w_down))
    xpb, xsb = xp.astype(BF16), xs.astype(BF16)
    new_p = ([], [], [], [], [], [])
    new_s = ([], [], [], [], [], [])
    for l in range(DEPTH):
        mix_w = (w_in_b[l], lower_bounds[l], hgrn_norm_g[l], mla_q_norm_g[l], mla_w_uq[l],
                 mla_kv_norm_g[l], mla_w_uk[l], mla_w_uv[l])
        ffn_w = (w_o_b[l], ln1_g[l], ln1_b[l], w_gate_b[l], w_up_b[l], w_down_b[l], ln2_g[l], ln2_b[l])
        mix_p, st_p = prompt_mixers(xpb, pos_p, *mix_w)
        mix_s, st_s = sample_mixers(xsb, pos_s, past, l, pool_bk, pool_bv, pool_bidx,
                                    cache_c_kv, pool_ckr, state_a[l], page_table, *mix_w)
        xp, xpb = finish_layer(xp, mix_p, *ffn_w)
        xs, xsb = finish_layer(xs, mix_s, *ffn_w)
        for lst, st in zip(new_p, st_p):
            lst.append(st)
        for lst, st in zip(new_s, st_s):
            lst.append(st)
    b_k_p, b_v_p, b_idx_p, c_kv_p, c_kr_p, s_a_p = [jnp.stack(v, 0) for v in new_p]
    b_k_s, b_v_s, b_idx_s, c_kv_s, c_kr_s, s_a_s = [jnp.stack(v, 0) for v in new_s]
    return (xp, xs, b_k_p, b_v_p, b_idx_p, c_kv_p, c_kr_p, s_a_p,
            b_k_s, b_v_s, b_idx_s, c_kv_s, c_kr_s, s_a_s)
```
